```python
import math
import jax, jax.numpy as jnp
from jax import lax
import numpy as np

D_MODEL = 1024
BATCH = 1
SEQ = 16384
DEPTH = 2
DEC_BATCH = 32
DEC_SEQ = 1
PAST_LEN = 16384
PAGE_SIZE = 128

HA = 8
DH_A = 64
DV_A = 2 * DH_A
HR = 4
DK_R = D_MODEL // HR
DV_R = 2 * D_MODEL // HR
D_FF = int(math.ceil(8 * D_MODEL / 3 / 256)) * 256
D_PLE = 256
QBLOCK = 128
RET_CHUNK = 128
ALPHA = (2 * DEPTH) ** 0.25
BETA = (8 * DEPTH) ** -0.25
EPS = 1e-5
NEG = -1e30
SPLIT_SIZES = (HA * 2 * DH_A, HA * 2 * DH_A, HA * DV_A, HR * DK_R, HR * DK_R,
               HR * DV_R, HR * DV_R, D_MODEL, D_MODEL)
N_IN = sum(SPLIT_SIZES)

kernel_name = "hybrid_diffattn_retention_decoder_step"


def split_cols(t, sizes):
    offs, c = [], 0
    for s in sizes[:-1]:
        c += s
        offs.append(c)
    return jnp.split(t, offs, axis=-1)


def layer_norm(x, g, b):
    xf = x.astype(jnp.float32)
    mu = jnp.mean(xf, -1, keepdims=True)
    var = jnp.mean(jnp.square(xf - mu), -1, keepdims=True)
    return ((xf - mu) * lax.rsqrt(var + EPS) * g + b).astype(x.dtype)


def alibi_slopes():
    return 2.0 ** (-8.0 * (jnp.arange(HA, dtype=jnp.float32) + 1.0) / HA)


def diff_scores(q, k, q_pos, k_pos, slopes):
    s = jnp.einsum('bqhcd,bkhcd->bhcqk', q, k).astype(jnp.float32) * (DH_A ** -0.5)
    dist = q_pos[:, None] - k_pos[None, :]
    s = s - slopes[None, :, None, None, None] * dist.astype(jnp.float32)
    return jnp.where(dist >= 0, s, NEG)


def diff_combine(p, lam):
    return p[:, :, 0] - lam * p[:, :, 1]


def diff_attn_prompt(q, k, v, lam, slopes):
    B, S = q.shape[0], q.shape[1]
    k_pos = jnp.arange(S)

    def blk(i):
        start = i * QBLOCK
        qb = lax.dynamic_slice_in_dim(q, start, QBLOCK, axis=1)
        q_pos = start + jnp.arange(QBLOCK)
        p = jax.nn.softmax(diff_scores(qb, k, q_pos, k_pos, slopes), axis=-1)
        a = diff_combine(p, lam).astype(v.dtype)
        return jnp.einsum('bhqk,bkhe->bqhe', a, v)

    o = lax.map(blk, jnp.arange(S // QBLOCK))
    return o.transpose(1, 0, 2, 3, 4).reshape(B, S, HA, DV_A)


def diff_attn_sample(q, k_new, v_new, k_past, v_past, lam, slopes):
    past, sd = k_past.shape[1], q.shape[1]
    q_pos = past + jnp.arange(sd)
    s = jnp.concatenate([diff_scores(q, k_past, q_pos, jnp.arange(past), slopes),
                         diff_scores(q, k_new, q_pos, q_pos, slopes)], axis=-1)
    a = diff_combine(jax.nn.softmax(s, axis=-1), lam).astype(v_new.dtype)
    return (jnp.einsum('bhqk,bkhe->bqhe', a[..., :past], v_past)
            + jnp.einsum('bhqk,bkhe->bqhe', a[..., past:], v_new))


def retention_chunk(q, k, v, s0, log_g):
    L = q.shape[1]
    idx = jnp.arange(L, dtype=jnp.float32)
    rel = idx[:, None] - idx[None, :]
    decay = jnp.where(rel >= 0, jnp.exp(log_g[:, None, None] * jnp.maximum(rel, 0.0)), 0.0)
    qf, kf, vf = q.astype(jnp.float32), k.astype(jnp.float32), v.astype(jnp.float32)
    s0 = s0.astype(jnp.float32)
    scores = jnp.einsum('bihd,bjhd->bhij', qf, kf) * decay[None]
    cross = jnp.exp(log_g[None, :] * (idx[:, None] + 1.0))[None, :, :, None]
    o = jnp.einsum('bhij,bjhe->bihe', scores, vf) + jnp.einsum('bihd,bhde->bihe', qf, s0) * cross
    k_dec = kf * jnp.exp(log_g[None, :] * (L - 1.0 - idx[:, None]))[None, :, :, None]
    s1 = jnp.exp(log_g * L)[None, :, None, None] * s0 + jnp.einsum('bjhd,bjhe->bhde', k_dec, vf)
    return o, s1


def retention_prompt(q, k, v, log_g):
    B, S = q.shape[0], q.shape[1]
    n = S // RET_CHUNK

    def to_chunks(t):
        return t.reshape(B, n, RET_CHUNK, *t.shape[2:]).swapaxes(0, 1)

    def step(s, xs):
        o, s = retention_chunk(xs[0], xs[1], xs[2], s, log_g)
        return s, o

    s0 = jnp.zeros((B, HR, DK_R, DV_R), jnp.float32)
    s_fin, o = lax.scan(step, s0, (to_chunks(q), to_chunks(k), to_chunks(v)))
    return o.swapaxes(0, 1).reshape(B, S, HR, DV_R), s_fin


def head_rmsnorm(o, g):
    of = o.astype(jnp.float32)
    return of * lax.rsqrt(jnp.mean(jnp.square(of), -1, keepdims=True) + EPS) * g


def head_groupnorm(o, g, b):
    of = o.astype(jnp.float32)
    mu = jnp.mean(of, -1, keepdims=True)
    var = jnp.mean(jnp.square(of - mu), -1, keepdims=True)
    n = (of - mu) * lax.rsqrt(var + EPS)
    return n.reshape(*o.shape[:-2], -1) * g + b


def trunk_layer(h, p, attend, retain, w_in, lam_init, attn_g, ret_g, ret_b, w_pa, w_pr, w_o,
                ln1_g, ln1_b, ln2_g, ln2_b, w_ffn_in, w_ffn_out, w_ple_gate, w_ple_proj):
    B, S, _ = h.shape
    qa, ka, va, qr, kr, vr, gr, ga, gb = split_cols(h @ w_in, SPLIT_SIZES)
    qa = qa.reshape(B, S, HA, 2, DH_A)
    ka = ka.reshape(B, S, HA, 2, DH_A)
    va = va.reshape(B, S, HA, DV_A)
    qr = qr.reshape(B, S, HR, DK_R)
    kr = kr.reshape(B, S, HR, DK_R) * (DK_R ** -0.5)
    vr = vr.reshape(B, S, HR, DV_R)
    o_a = attend(qa, ka, va)
    o_r, ret_state = retain(qr, kr, vr)
    y_a = (head_rmsnorm(o_a, attn_g) * (1.0 - lam_init)).reshape(B, S, -1).astype(h.dtype) @ w_pa
    y_r = (head_groupnorm(o_r, ret_g, ret_b) * jax.nn.silu(gr)).astype(h.dtype) @ w_pr
    y_mix = (jax.nn.sigmoid(ga) * y_a + jax.nn.sigmoid(gb) * y_r).astype(h.dtype) @ w_o
    x = layer_norm(ALPHA * h + y_mix, ln1_g, ln1_b)
    u, g = jnp.split(x @ w_ffn_in, 2, axis=-1)
    x = layer_norm(ALPHA * x + (jax.nn.silu(u) * g) @ w_ffn_out, ln2_g, ln2_b)
    x = x + jax.nn.sigmoid(x @ w_ple_gate) * (p @ w_ple_proj)
    return x.astype(h.dtype), ka.reshape(B, S, HA, 2 * DH_A), va, ret_state


def setup_inputs(seed: int = 0) -> dict:
    key = jax.random.key(seed)
    ks = jax.random.split(key, 32)
    n_pages = PAST_LEN // PAGE_SIZE
    n_used = DEC_BATCH * n_pages
    n_phys = n_used + (n_used + 3) // 4
    f32 = jnp.float32
    nrm = lambda k, shape, s: jax.random.normal(k, shape, f32) * s
    col_scale = jnp.concatenate([jnp.full((s,), BETA if j in (2, 5) else 1.0, f32)
                                 for j, s in enumerate(SPLIT_SIZES)])
    page_table = jax.random.permutation(ks[5], n_phys)[:n_used].reshape(DEC_BATCH, n_pages).astype(jnp.int32)
    return {
        "x_prompt": nrm(ks[0], (BATCH, SEQ, D_MODEL), 1.0),
        "x_sample": nrm(ks[1], (DEC_BATCH, DEC_SEQ, D_MODEL), 1.0),
        "cache_k": nrm(ks[2], (DEPTH, n_phys, PAGE_SIZE, HA, 2 * DH_A), 1.0),
        "cache_v": nrm(ks[3], (DEPTH, n_phys, PAGE_SIZE, HA, DV_A), 1.0),
        "state_ret": nrm(ks[4], (DEPTH, DEC_BATCH, HR, DK_R, DV_R), 0.05),
        "page_table": page_table,
        "p_prompt": nrm(ks[6], (DEPTH, BATCH, SEQ, D_PLE), 1.0),
        "p_sample": nrm(ks[7], (DEPTH, DEC_BATCH, DEC_SEQ, D_PLE), 1.0),
        "w_in": nrm(ks[8], (DEPTH, D_MODEL, N_IN), D_MODEL ** -0.5) * col_scale,
        "lambda_q1": nrm(ks[9], (DEPTH, DH_A), 0.1),
        "lambda_k1": nrm(ks[10], (DEPTH, DH_A), 0.1),
        "lambda_q2": nrm(ks[11], (DEPTH, DH_A), 0.1),
        "lambda_k2": nrm(ks[12], (DEPTH, DH_A), 0.1),
        "attn_norm_g": 1.0 + nrm(ks[13], (DEPTH, DV_A), 0.02),
        "ret_norm_g": 1.0 + nrm(ks[14], (DEPTH, HR * DV_R), 0.02),
        "ret_norm_b": nrm(ks[15], (DEPTH, HR * DV_R), 0.02),
        "w_pa": nrm(ks[16], (DEPTH, HA * DV_A, D_MODEL), (HA * DV_A) ** -0.5),
        "w_pr": nrm(ks[17], (DEPTH, HR * DV_R, D_MODEL), (HR * DV_R) ** -0.5),
        "w_o": nrm(ks[18], (DEPTH, D_MODEL, D_MODEL), BETA * D_MODEL ** -0.5),
        "ln1_g": 1.0 + nrm(ks[19], (DEPTH, D_MODEL), 0.02),
        "ln1_b": nrm(ks[20], (DEPTH, D_MODEL), 0.02),
        "ln2_g": 1.0 + nrm(ks[21], (DEPTH, D_MODEL), 0.02),
        "ln2_b": nrm(ks[22], (DEPTH, D_MODEL), 0.02),
        "w_ffn_in": nrm(ks[23], (DEPTH, D_MODEL, 2 * D_FF), D_MODEL ** -0.5),
        "w_ffn_out": nrm(ks[24], (DEPTH, D_FF, D_MODEL), BETA * D_FF ** -0.5),
        "w_ple_gate": nrm(ks[25], (DEPTH, D_MODEL, D_MODEL), D_MODEL ** -0.5),
        "w_ple_proj": nrm(ks[26], (DEPTH, D_PLE, D_MODEL), BETA * D_PLE ** -0.5),
    }


def reference(x_prompt, x_sample, cache_k, cache_v, state_ret, page_table, p_prompt, p_sample,
              w_in, lambda_q1, lambda_k1, lambda_q2, lambda_k2, attn_norm_g, ret_norm_g, ret_norm_b,
              w_pa, w_pr, w_o, ln1_g, ln1_b, ln2_g, ln2_b, w_ffn_in, w_ffn_out, w_ple_gate, w_ple_proj):
    slopes = alibi_slopes()
    log_g = jnp.log(1.0 - 2.0 ** (-5.0 - jnp.arange(HR, dtype=jnp.float32)))
    db = x_sample.shape[0]
    past = page_table.shape[1] * PAGE_SIZE
    hp, hs = x_prompt, x_sample
    kp_l, vp_l, rp_l, ks_l, vs_l, rs_l = [], [], [], [], [], []
    for i in range(DEPTH):
        lam_init = 0.8 - 0.6 * math.exp(-0.3 * i)
        lam = (jnp.exp(jnp.sum(lambda_q1[i].astype(jnp.float32) * lambda_k1[i].astype(jnp.float32)))
               - jnp.exp(jnp.sum(lambda_q2[i].astype(jnp.float32) * lambda_k2[i].astype(jnp.float32)))
               + lam_init)
        k_past = cache_k[i][page_table].reshape(db, past, HA, 2, DH_A)
        v_past = cache_v[i][page_table].reshape(db, past, HA, DV_A)
        s_past = state_ret[i]
        weights = (w_in[i], lam_init, attn_norm_g[i], ret_norm_g[i], ret_norm_b[i], w_pa[i], w_pr[i],
                   w_o[i], ln1_g[i], ln1_b[i], ln2_g[i], ln2_b[i], w_ffn_in[i], w_ffn_out[i],
                   w_ple_gate[i], w_ple_proj[i])
        attend_p = lambda q, k, v, lam=lam: diff_attn_prompt(q, k, v, lam, slopes)
        retain_p = lambda q, k, v: retention_prompt(q, k, v, log_g)
        attend_s = lambda q, k, v, lam=lam, kp=k_past, vp=v_past: diff_attn_sample(q, k, v, kp, vp, lam, slopes)
        retain_s = lambda q, k, v, s0=s_past: retention_chunk(q, k, v, s0, log_g)
        hp, kp, vp, rp = trunk_layer(hp, p_prompt[i], attend_p, retain_p, *weights)
        hs, ksn, vsn, rsn = trunk_layer(hs, p_sample[i], attend_s, retain_s, *weights)
        kp_l.append(kp); vp_l.append(vp); rp_l.append(rp)
        ks_l.append(ksn); vs_l.append(vsn); rs_l.append(rsn)
    return (hp, hs, jnp.stack(kp_l), jnp.stack(vp_l), jnp.stack(rp_l),
            jnp.stack(ks_l), jnp.stack(vs_l), jnp.stack(rs_l))
```

```python
import functools
import math

import jax
import jax.numpy as jnp
from jax import lax
from jax.experimental import pallas as pl
from jax.experimental.pallas import tpu as pltpu

F32 = jnp.float32
BF16 = jnp.bfloat16

HA = 8
DH_A = 64
DV_A = 128
HR = 4
PAGE_SIZE = 128
EPS = 1e-5
NEG = -1e30

LANES = 128
V7X_VMEM_BYTES = 64 * 1024 * 1024
VMEM_LIMIT = V7X_VMEM_BYTES - 8 * 1024 * 1024


def _cparams(sem):
    return pltpu.CompilerParams(dimension_semantics=sem, vmem_limit_bytes=VMEM_LIMIT)


def _sigmoid(x):
    return 1.0 / (1.0 + jnp.exp(-x))


def _layer_norm(x, g, b):
    mu = jnp.mean(x, axis=-1, keepdims=True)
    xc = x - mu
    var = jnp.mean(xc * xc, axis=-1, keepdims=True)
    return xc * lax.rsqrt(var + EPS) * g + b


def _dot(a, b):
    return jnp.dot(a, b, preferred_element_type=F32)


def _dot_nt(a, b):
    return lax.dot_general(a, b, (((1,), (1,)), ((), ())), preferred_element_type=F32)


def _row_tile(m, target):
    t = min(m, target)
    assert m % t == 0, (m, t)
    return t


def _qkv_kernel(x_ref, w_ref, q_ref, kf_ref, vf_ref, *flash_refs, d, q_scale):
    xb = x_ref[...].astype(BF16)
    q = _dot(xb, w_ref[:, 0:d])
    q_ref[...] = (q * q_scale).astype(q_ref.dtype)
    k = _dot(xb, w_ref[:, d:2 * d])
    kf_ref[...] = k
    v = _dot(xb, w_ref[:, 2 * d:3 * d])
    vf_ref[...] = v
    if flash_refs:
        kb_ref, vt_ref = flash_refs
        kb_ref[...] = k.astype(BF16)
        for h in range(HA):
            vt_ref[h] = v[:, h * DV_A:(h + 1) * DV_A].T.astype(BF16)


def _qkv_proj(x, w_bf, tm, for_flash):
    m, d = x.shape
    tm = _row_tile(m, tm)
    kern = functools.partial(_qkv_kernel, d=d, q_scale=DH_A ** -0.5)
    row = lambda i: (i, 0)
    out_specs = [pl.BlockSpec((tm, d), row), pl.BlockSpec((tm, d), row), pl.BlockSpec((tm, d), row)]
    out_shape = [jax.ShapeDtypeStruct((m, d), BF16 if for_flash else F32),
                 jax.ShapeDtypeStruct((m, d), F32), jax.ShapeDtypeStruct((m, d), F32)]
    if for_flash:
        out_specs += [pl.BlockSpec((tm, d), row), pl.BlockSpec((HA, DV_A, tm), lambda i: (0, 0, i))]
        out_shape += [jax.ShapeDtypeStruct((m, d), BF16), jax.ShapeDtypeStruct((HA, DV_A, m), BF16)]
    return pl.pallas_call(
        kern,
        grid=(m // tm,),
        in_specs=[pl.BlockSpec((tm, d), row),
                  pl.BlockSpec((d, 3 * d), lambda i: (0, 0))],
        out_specs=out_specs,
        out_shape=out_shape,
        compiler_params=_cparams(("arbitrary",)),
        name="qkv_proj",
    )(x, w_bf)


def _proj_kernel(x_ref, w_ref, s_ref, o_ref, xb_ref):
    @pl.when(pl.program_id(1) == 0)
    def _():
        xb_ref[...] = x_ref[...].astype(BF16)

    o_ref[...] = (_dot(xb_ref[...], w_ref[...]) * s_ref[...]).astype(o_ref.dtype)


def _proj(x, w_bf, col_scale, col0, n, tm, tn, out_dtype=F32):
    m, d = x.shape
    tm = _row_tile(m, tm)
    assert n % tn == 0 and col0 % tn == 0
    jb = col0 // tn
    return pl.pallas_call(
        _proj_kernel,
        grid=(m // tm, n // tn),
        in_specs=[pl.BlockSpec((tm, d), lambda i, j: (i, 0)),
                  pl.BlockSpec((d, tn), lambda i, j: (0, jb + j)),
                  pl.BlockSpec((1, tn), lambda i, j: (0, j))],
        out_specs=pl.BlockSpec((tm, tn), lambda i, j: (i, j)),
        out_shape=jax.ShapeDtypeStruct((m, n), out_dtype),
        scratch_shapes=[pltpu.VMEM((tm, d), BF16)],
        compiler_params=_cparams(("parallel", "arbitrary")),
        name="col_proj",
    )(x, w_bf, col_scale)


def _flash_kernel(slopes_ref, lam_ref, q_ref, k_ref, vt_ref, g_ref, o_ref,
                  bias_ref, biasd_ref, st_ref, acc_ref, *, t, out_scale):
    h = pl.program_id(0)
    qi = pl.program_id(1)
    slope = slopes_ref[h]

    @pl.when(qi == 0)
    def _():
        jj = lax.broadcasted_iota(jnp.int32, (t, t), 0)
        ii = lax.broadcasted_iota(jnp.int32, (t, t), 1)
        b = (jj - ii).astype(F32) * slope
        bias_ref[...] = b
        biasd_ref[...] = jnp.where(ii >= jj, b, NEG)

    q = q_ref[...]
    lane = lax.broadcasted_iota(jnp.int32, q.shape, 1)
    zero = jnp.zeros_like(q)
    qm = (jnp.where(lane < DH_A, q, zero), jnp.where(lane >= DH_A, q, zero))

    st_ref[0:1, :] = jnp.full((1, t), -jnp.inf, F32)
    st_ref[1:2, :] = jnp.zeros((1, t), F32)
    st_ref[2:3, :] = jnp.full((1, t), -jnp.inf, F32)
    st_ref[3:4, :] = jnp.zeros((1, t), F32)
    acc_ref[...] = jnp.zeros_like(acc_ref)

    def step(kb, b_ref, cterm):
        k0 = pl.multiple_of(kb * t, t)
        k_blk = k_ref[pl.ds(k0, t), :]
        vt_blk = vt_ref[:, pl.ds(k0, t)]
        for c in range(2):
            s = _dot_nt(k_blk, qm[c]) + b_ref[...]
            m_old = st_ref[2 * c:2 * c + 1, :]
            l_old = st_ref[2 * c + 1:2 * c + 2, :]
            m_new = jnp.maximum(m_old, jnp.max(s, axis=0, keepdims=True) - cterm)
            alpha = jnp.exp(m_old - m_new)
            p = jnp.exp(s - (m_new + cterm))
            st_ref[2 * c:2 * c + 1, :] = m_new
            st_ref[2 * c + 1:2 * c + 2, :] = alpha * l_old + jnp.sum(p, axis=0, keepdims=True)
            acc_ref[c] = alpha * acc_ref[c] + _dot(vt_blk, p.astype(BF16))

    def body(kb, carry):
        step(kb, bias_ref, slope * ((qi - kb) * t).astype(F32))
        return carry

    lax.fori_loop(0, qi, body, 0)
    step(qi, biasd_ref, jnp.float32(0.0))

    lam = lam_ref[0]
    o1 = acc_ref[0] / st_ref[1:2, :]
    o2 = acc_ref[1] / st_ref[3:4, :]
    o = o1 - lam * o2
    ms = jnp.mean(o * o, axis=0, keepdims=True)
    y = o * lax.rsqrt(ms + EPS) * (g_ref[...] * out_scale)
    o_ref[...] = y.T.astype(o_ref.dtype)


def _flash_attention(q_bf, k_bf, vt_bf, slopes, lam, g_col, out_scale, t):
    s, d = q_bf.shape
    t = _row_tile(s, t)
    kern = functools.partial(_flash_kernel, t=t, out_scale=out_scale)
    smem = pl.BlockSpec(memory_space=pltpu.SMEM)
    return pl.pallas_call(
        kern,
        grid=(HA, s // t),
        in_specs=[smem, smem,
                  pl.BlockSpec((t, DV_A), lambda h, i: (i, h)),
                  pl.BlockSpec((s, DV_A), lambda h, i: (0, h)),
                  pl.BlockSpec((None, DV_A, s), lambda h, i: (h, 0, 0)),
                  pl.BlockSpec((DV_A, 1), lambda h, i: (0, 0))],
        out_specs=pl.BlockSpec((t, DV_A), lambda h, i: (i, h)),
        out_shape=jax.ShapeDtypeStruct((s, d), BF16),
        scratch_shapes=[pltpu.VMEM((t, t), F32), pltpu.VMEM((t, t), F32),
                        pltpu.VMEM((8, t), F32), pltpu.VMEM((2, DV_A, t), F32)],
        compiler_params=_cparams(("arbitrary", "arbitrary")),
        name="diff_flash_attn",
    )(slopes, lam, q_bf, k_bf, vt_bf, g_col)


def _retention_kernel(rc_ref, q_ref, k_ref, v_ref, gr_ref, gam_ref, bet_ref, y_ref, s_ref,
                      decay_ref, cross_ref, kdec_ref, *, L):
    h = pl.program_id(0)
    c = pl.program_id(1)
    lg = rc_ref[0, h]
    g_chunk = rc_ref[1, h]

    @pl.when(c == 0)
    def _():
        ii = lax.broadcasted_iota(jnp.int32, (L, L), 0)
        jj = lax.broadcasted_iota(jnp.int32, (L, L), 1)
        rel = (ii - jj).astype(F32)
        decay_ref[...] = jnp.where(rel >= 0, jnp.exp(lg * jnp.maximum(rel, 0.0)), 0.0)
        idx = lax.broadcasted_iota(jnp.int32, (L, 1), 0).astype(F32)
        cross_ref[...] = jnp.exp(lg * (idx + 1.0))
        kdec_ref[...] = jnp.exp(lg * (L - 1.0 - idx))
        s_ref[...] = jnp.zeros_like(s_ref)

    q = q_ref[...]
    k = k_ref[...]
    vb = v_ref[...].astype(BF16)
    qb = q.astype(BF16)
    scores = _dot_nt(qb, k.astype(BF16)) * decay_ref[...]
    s0 = s_ref[...]
    o = _dot(scores.astype(BF16), vb) + _dot(qb, s0.astype(BF16)) * cross_ref[...]
    kd_t = (k * kdec_ref[...]).T.astype(BF16)
    s_ref[...] = g_chunk * s0 + _dot(kd_t, vb)

    mu = jnp.mean(o, axis=-1, keepdims=True)
    oc = o - mu
    var = jnp.mean(oc * oc, axis=-1, keepdims=True)
    n = oc * lax.rsqrt(var + EPS) * gam_ref[...] + bet_ref[...]
    gr = gr_ref[...]
    y_ref[...] = (n * (gr * _sigmoid(gr))).astype(y_ref.dtype)


def _retention_prompt(rg, ret_consts, gam, bet, L):
    s = rg.shape[0]
    d = rg.shape[1] // 8
    dk = d // HR
    dv = 2 * d // HR
    L = _row_tile(s, L)
    kern = functools.partial(_retention_kernel, L=L)
    return pl.pallas_call(
        kern,
        grid=(HR, s // L),
        in_specs=[pl.BlockSpec(memory_space=pltpu.SMEM),
                  pl.BlockSpec((L, dk), lambda h, c: (c, h)),
                  pl.BlockSpec((L, dk), lambda h, c: (c, HR + h)),
                  pl.BlockSpec((L, dv), lambda h, c: (c, HR + h)),
                  pl.BlockSpec((L, dv), lambda h, c: (c, 2 * HR + h)),
                  pl.BlockSpec((1, dv), lambda h, c: (0, h)),
                  pl.BlockSpec((1, dv), lambda h, c: (0, h))],
        out_specs=[pl.BlockSpec((L, dv), lambda h, c: (c, h)),
                   pl.BlockSpec((None, dk, dv), lambda h, c: (h, 0, 0))],
        out_shape=[jax.ShapeDtypeStruct((s, 2 * d), BF16),
                   jax.ShapeDtypeStruct((HR, dk, dv), F32)],
        scratch_shapes=[pltpu.VMEM((L, L), F32), pltpu.VMEM((L, 1), F32), pltpu.VMEM((L, 1), F32)],
        compiler_params=_cparams(("arbitrary", "arbitrary")),
        name="retention_scan",
    )(ret_consts, rg, rg, rg, rg, gam, bet)


def _retention_step_kernel(q_ref, k_ref, v_ref, gr_ref, s0_ref, gam_ref, bet_ref, y_ref, s1_ref):
    for h in range(HR):
        g = 1.0 - 2.0 ** (-5.0 - h)
        q = q_ref[h]
        k = k_ref[h]
        v = v_ref[h]
        s0 = s0_ref[h]
        qk = jnp.sum(q * k, axis=0, keepdims=True)
        o = qk * v + jnp.sum(q * s0, axis=0, keepdims=True) * g
        s1_ref[h] = g * s0 + k * v
        mu = jnp.mean(o, axis=-1, keepdims=True)
        oc = o - mu
        var = jnp.mean(oc * oc, axis=-1, keepdims=True)
        n = oc * lax.rsqrt(var + EPS) * gam_ref[h] + bet_ref[h]
        gr = gr_ref[h]
        y_ref[h] = (n * (gr * _sigmoid(gr))).astype(y_ref.dtype)


def _retention_step(q_col, k_col, v_row, gr_row, s0, gam, bet):
    b, _, dk, dv = s0.shape
    per_b = lambda i: (i, 0, 0, 0)
    return pl.pallas_call(
        _retention_step_kernel,
        grid=(b,),
        in_specs=[pl.BlockSpec((None, HR, dk, 1), per_b),
                  pl.BlockSpec((None, HR, dk, 1), per_b),
                  pl.BlockSpec((None, HR, 1, dv), per_b),
                  pl.BlockSpec((None, HR, 1, dv), per_b),
                  pl.BlockSpec((None, HR, dk, dv), per_b),
                  pl.BlockSpec((HR, 1, dv), lambda i: (0, 0, 0)),
                  pl.BlockSpec((HR, 1, dv), lambda i: (0, 0, 0))],
        out_specs=[pl.BlockSpec((None, HR, 1, dv), per_b),
                   pl.BlockSpec((None, HR, dk, dv), per_b)],
        out_shape=[jax.ShapeDtypeStruct((b, HR, 1, dv), BF16),
                   jax.ShapeDtypeStruct((b, HR, dk, dv), F32)],
        compiler_params=_cparams(("arbitrary",)),
        name="retention_step",
    )(q_col, k_col, v_row, gr_row, s0, gam, bet)


def _decode_kernel(pt_ref, lam_ref, qcol_ref, knew_ref, vnew_ref, g_ref, *rest,
                   npp, n_steps, past, out_scale):
    k_refs = rest[:npp]
    v_refs = rest[npp:2 * npp]
    o_ref = rest[2 * npp]
    wq_ref, e_ref, sc_ref, a_ref, acc_ref, self_ref = rest[2 * npp + 1:]
    s = pl.program_id(1)
    d = HA * DV_A
    blk = npp * PAGE_SIZE

    @pl.when(s == 0)
    def _():
        r = lax.broadcasted_iota(jnp.int32, (d, LANES), 0)
        n = lax.broadcasted_iota(jnp.int32, (d, LANES), 1)
        wq = jnp.where(n == r // DH_A, qcol_ref[...], 0.0).astype(BF16)
        wq_ref[...] = wq
        n2 = lax.broadcasted_iota(jnp.int32, (LANES, d), 0)
        c2 = lax.broadcasted_iota(jnp.int32, (LANES, d), 1)
        e_ref[...] = jnp.where(n2 == 2 * (c2 // DV_A), 1.0, 0.0).astype(BF16)
        kn = jnp.broadcast_to(knew_ref[...], (8, d)).astype(BF16)
        self_ref[...] = _dot(kn, wq)

    @pl.when(s < n_steps)
    def _():
        for r in range(npp):
            kp = k_refs[r][...].astype(BF16)
            row0 = pl.multiple_of(s * blk + r * PAGE_SIZE, PAGE_SIZE)
            sc_ref[pl.ds(row0, PAGE_SIZE), :] = _dot(kp, wq_ref[...])

    @pl.when(s == n_steps)
    def _():
        lane = lax.broadcasted_iota(jnp.int32, (1, LANES), 1)
        slope = jnp.exp2(((lane // 2) + 1).astype(F32) * (-8.0 / HA))
        lam = lam_ref[0]
        ch = 512
        nch = past // ch

        def bias(i):
            t = lax.broadcasted_iota(jnp.int32, (ch, LANES), 0) + i * ch
            return (t - past).astype(F32) * slope

        def max_body(i, m):
            r0 = pl.multiple_of(i * ch, ch)
            sv = sc_ref[pl.ds(r0, ch), :] + bias(i)
            sc_ref[pl.ds(r0, ch), :] = sv
            return jnp.maximum(m, jnp.max(sv, axis=0, keepdims=True))

        s_self = self_ref[0:1, :]
        m = lax.fori_loop(0, nch, max_body, s_self)

        def sum_body(i, l):
            r0 = pl.multiple_of(i * ch, ch)
            p = jnp.exp(sc_ref[pl.ds(r0, ch), :] - m)
            sc_ref[pl.ds(r0, ch), :] = p
            return l + jnp.sum(p, axis=0, keepdims=True)

        p_self = jnp.exp(s_self - m)
        l = lax.fori_loop(0, nch, sum_body, p_self)
        inv = 1.0 / l

        def comb_body(i, carry):
            r0 = pl.multiple_of(i * ch, ch)
            pn = sc_ref[pl.ds(r0, ch), :] * inv
            a = pn - lam * pltpu.roll(pn, LANES - 1, 1)
            a_ref[pl.ds(r0, ch), :] = a.astype(BF16)
            return carry

        lax.fori_loop(0, nch, comb_body, 0)
        pn_self = p_self * inv
        a_self = pn_self - lam * pltpu.roll(pn_self, LANES - 1, 1)
        a_self_x = _dot(jnp.broadcast_to(a_self, (8, LANES)).astype(BF16), e_ref[...])
        acc_ref[...] = jnp.zeros_like(acc_ref)
        acc_ref[0:1, :] = a_self_x[0:1, :] * vnew_ref[...]

    @pl.when(s >= n_steps)
    def _():
        acc = acc_ref[...]
        for r in range(npp):
            row0 = pl.multiple_of((s - n_steps) * blk + r * PAGE_SIZE, PAGE_SIZE)
            a_x = _dot(a_ref[pl.ds(row0, PAGE_SIZE), :], e_ref[...])
            prod = a_x * v_refs[r][...]
            acc = acc + jnp.sum(prod.reshape(PAGE_SIZE // 8, 8, d), axis=0)
        acc_ref[...] = acc

    @pl.when(s == 2 * n_steps - 1)
    def _():
        o = jnp.sum(acc_ref[...], axis=0, keepdims=True)
        g = g_ref[...] * out_scale
        for h in range(HA):
            oh = o[:, h * DV_A:(h + 1) * DV_A]
            ms = jnp.mean(oh * oh, axis=-1, keepdims=True)
            o_ref[:, h * DV_A:(h + 1) * DV_A] = (oh * lax.rsqrt(ms + EPS) * g).astype(o_ref.dtype)


def _decode_attention(layer, cache_k, cache_v, page_table, lam, q_col, k_new, v_new, g_row, out_scale, npp):
    b, n_pages = page_table.shape
    d = HA * DV_A
    assert n_pages % npp == 0
    n_steps = n_pages // npp
    past = n_pages * PAGE_SIZE
    kern = functools.partial(_decode_kernel, npp=npp, n_steps=n_steps, past=past, out_scale=out_scale)

    def k_map(r):
        return lambda bi, s, pt: (layer, pt[bi, jnp.minimum(s, n_steps - 1) * npp + r], 0, 0)

    def v_map(r):
        return lambda bi, s, pt: (layer, pt[bi, jnp.maximum(s - n_steps, 0) * npp + r], 0, 0)

    page_block = (None, None, PAGE_SIZE, d)
    per_b = lambda bi, s, pt: (bi, 0, 0)
    grid_spec = pltpu.PrefetchScalarGridSpec(
        num_scalar_prefetch=1,
        grid=(b, 2 * n_steps),
        in_specs=[pl.BlockSpec(memory_space=pltpu.SMEM),
                  pl.BlockSpec((None, d, 1), per_b),
                  pl.BlockSpec((None, 1, d), per_b),
                  pl.BlockSpec((None, 1, d), per_b),
                  pl.BlockSpec((1, DV_A), lambda bi, s, pt: (0, 0))]
                 + [pl.BlockSpec(page_block, k_map(r)) for r in range(npp)]
                 + [pl.BlockSpec(page_block, v_map(r)) for r in range(npp)],
        out_specs=pl.BlockSpec((None, 1, d), per_b),
        scratch_shapes=[pltpu.VMEM((d, LANES), BF16), pltpu.VMEM((LANES, d), BF16),
                        pltpu.VMEM((past, LANES), F32), pltpu.VMEM((past, LANES), BF16),
                        pltpu.VMEM((8, d), F32), pltpu.VMEM((8, LANES), F32)],
    )
    return pl.pallas_call(
        kern,
        grid_spec=grid_spec,
        out_shape=jax.ShapeDtypeStruct((b, 1, d), BF16),
        compiler_params=_cparams(("arbitrary", "arbitrary")),
        name="paged_decode_attn",
    )(page_table, lam, q_col, k_new, v_new, g_row, *([cache_k] * npp), *([cache_v] * npp))


def _merge_kernel(ya_ref, yr_ref, ga_ref, gb_ref, h_ref, wpa_ref, wpr_ref, wo_ref, g1_ref, b1_ref, o_ref, *, alpha):
    ya = _dot(ya_ref[...], wpa_ref[...])
    yr = _dot(yr_ref[...], wpr_ref[...])
    mix = (_sigmoid(ga_ref[...]) * ya + _sigmoid(gb_ref[...]) * yr).astype(BF16)
    x = alpha * h_ref[...] + _dot(mix, wo_ref[...])
    o_ref[...] = _layer_norm(x, g1_ref[...], b1_ref[...])


def _merge(y_a, y_r, rg, h, wpa, wpr, wo, g1, b1, alpha, tm):
    m, d = h.shape
    tm = _row_tile(m, tm)
    row = lambda i: (i, 0)
    const = lambda i: (0, 0)
    kern = functools.partial(_merge_kernel, alpha=alpha)
    return pl.pallas_call(
        kern,
        grid=(m // tm,),
        in_specs=[pl.BlockSpec((tm, d), row),
                  pl.BlockSpec((tm, 2 * d), row),
                  pl.BlockSpec((tm, d), lambda i: (i, 6)),
                  pl.BlockSpec((tm, d), lambda i: (i, 7)),
                  pl.BlockSpec((tm, d), row),
                  pl.BlockSpec((d, d), const),
                  pl.BlockSpec((2 * d, d), const),
                  pl.BlockSpec((d, d), const),
                  pl.BlockSpec((1, d), const),
                  pl.BlockSpec((1, d), const)],
        out_specs=pl.BlockSpec((tm, d), row),
        out_shape=jax.ShapeDtypeStruct((m, d), F32),
        compiler_params=_cparams(("parallel",)),
        name="branch_merge_ln1",
    )(y_a, y_r, rg, rg, h, wpa, wpr, wo, g1, b1)


def _ffn_kernel(x_ref, p_ref, win_ref, wout_ref, wg_ref, wp_ref, g2_ref, b2_ref, o_ref, *, alpha, dff, chunk):
    x = x_ref[...]
    xb = x.astype(BF16)
    acc = jnp.zeros(x.shape, F32)
    for c0 in range(0, dff, chunk):
        u = _dot(xb, win_ref[:, c0:c0 + chunk])
        g = _dot(xb, win_ref[:, dff + c0:dff + c0 + chunk])
        act = (u * _sigmoid(u) * g).astype(BF16)
        acc = acc + _dot(act, wout_ref[c0:c0 + chunk, :])
    x2 = _layer_norm(alpha * x + acc, g2_ref[...], b2_ref[...])
    gate = _sigmoid(_dot(x2.astype(BF16), wg_ref[...]))
    o_ref[...] = x2 + gate * _dot(p_ref[...].astype(BF16), wp_ref[...])


def _ffn(x, p, win, wout, wg, wp, g2, b2, alpha, tm):
    m, d = x.shape
    dff = wout.shape[0]
    dp = p.shape[1]
    tm = _row_tile(m, tm)
    chunk = dff // 2
    assert chunk % LANES == 0
    row = lambda i: (i, 0)
    const = lambda i: (0, 0)
    once = pl.Buffered(1)
    kern = functools.partial(_ffn_kernel, alpha=alpha, dff=dff, chunk=chunk)
    return pl.pallas_call(
        kern,
        grid=(m // tm,),
        in_specs=[pl.BlockSpec((tm, d), row),
                  pl.BlockSpec((tm, dp), row),
                  pl.BlockSpec((d, 2 * dff), const, pipeline_mode=once),
                  pl.BlockSpec((dff, d), const, pipeline_mode=once),
                  pl.BlockSpec((d, d), const, pipeline_mode=once),
                  pl.BlockSpec((dp, d), const, pipeline_mode=once),
                  pl.BlockSpec((1, d), const),
                  pl.BlockSpec((1, d), const)],
        out_specs=pl.BlockSpec((tm, d), row),
        out_shape=jax.ShapeDtypeStruct((m, d), F32),
        compiler_params=_cparams(("parallel",)),
        name="ffn_ln2_ple",
    )(x, p, win, wout, wg, wp, g2, b2)


def kernel(x_prompt, x_sample, cache_k, cache_v, state_ret, page_table, p_prompt, p_sample, w_in, lambda_q1, lambda_k1, lambda_q2, lambda_k2, attn_norm_g, ret_norm_g, ret_norm_b, w_pa, w_pr, w_o, ln1_g, ln1_b, ln2_g, ln2_b, w_ffn_in, w_ffn_out, w_ple_gate, w_ple_proj):
    depth = w_in.shape[0]
    bp, seq, d = x_prompt.shape
    db, dseq, _ = x_sample.shape
    assert bp == 1 and dseq == 1 and d == HA * DV_A
    dk = d // HR
    dv = 2 * d // HR
    alpha = (2 * depth) ** 0.25
    n_phys = cache_k.shape[1]

    slopes = 2.0 ** (-8.0 * (jnp.arange(HA, dtype=F32) + 1.0) / HA)
    log_g = jnp.log(1.0 - 2.0 ** (-5.0 - jnp.arange(HR, dtype=F32)))
    ret_chunk = min(seq, 256)
    ret_consts = jnp.stack([log_g, jnp.exp(log_g * ret_chunk)])
    rg_scale = jnp.concatenate([jnp.ones((d,), F32), jnp.full((d,), dk ** -0.5, F32),
                                jnp.ones((6 * d,), F32)]).reshape(1, 8 * d)
    cache_k2 = cache_k.reshape(depth, n_phys, PAGE_SIZE, d)
    cache_v2 = cache_v.reshape(depth, n_phys, PAGE_SIZE, d)
    page_table = page_table.astype(jnp.int32)

    hp = x_prompt.reshape(seq, d)
    hs = x_sample.reshape(db, d)
    kp_l, vp_l, rp_l, ks_l, vs_l, rs_l = [], [], [], [], [], []
    for i in range(depth):
        lam_init = 0.8 - 0.6 * math.exp(-0.3 * i)
        lam = (jnp.exp(jnp.sum(lambda_q1[i].astype(F32) * lambda_k1[i].astype(F32)))
               - jnp.exp(jnp.sum(lambda_q2[i].astype(F32) * lambda_k2[i].astype(F32)))
               + lam_init).reshape(1)
        out_scale = 1.0 - lam_init
        w_in_bf = w_in[i].astype(BF16)
        wpa, wpr, wo = w_pa[i].astype(BF16), w_pr[i].astype(BF16), w_o[i].astype(BF16)
        wfi, wfo = w_ffn_in[i].astype(BF16), w_ffn_out[i].astype(BF16)
        wpg, wpp = w_ple_gate[i].astype(BF16), w_ple_proj[i].astype(BF16)
        g1, b1 = ln1_g[i].reshape(1, d), ln1_b[i].reshape(1, d)
        g2, b2 = ln2_g[i].reshape(1, d), ln2_b[i].reshape(1, d)
        gam, bet = ret_norm_g[i].reshape(1, 2 * d), ret_norm_b[i].reshape(1, 2 * d)
        attn_g = attn_norm_g[i]

        q_bf, k_f, v_f, k_bf, vt_bf = _qkv_proj(hp, w_in_bf, tm=512, for_flash=True)
        rg = _proj(hp, w_in_bf, rg_scale, 3 * d, 8 * d, tm=1024, tn=1024)
        y_a = _flash_attention(q_bf, k_bf, vt_bf, slopes, lam, attn_g.reshape(DV_A, 1), out_scale, t=512)
        y_r, ret_p = _retention_prompt(rg, ret_consts, gam, bet, L=ret_chunk)
        x1 = _merge(y_a, y_r, rg, hp, wpa, wpr, wo, g1, b1, alpha, tm=512)
        hp = _ffn(x1, p_prompt[i].reshape(seq, -1), wfi, wfo, wpg, wpp, g2, b2, alpha, tm=512)
        kp_l.append(k_f.reshape(1, seq, HA, DV_A))
        vp_l.append(v_f.reshape(1, seq, HA, DV_A))
        rp_l.append(ret_p.reshape(1, HR, dk, dv))

        q_s, ks_f, vs_f = _qkv_proj(hs, w_in_bf, tm=db, for_flash=False)
        rg_s = _proj(hs, w_in_bf, rg_scale, 3 * d, 8 * d, tm=db, tn=1024)
        ya_s = _decode_attention(i, cache_k2, cache_v2, page_table, lam, q_s.reshape(db, d, 1),
                                 ks_f.reshape(db, 1, d), vs_f.reshape(db, 1, d),
                                 attn_g.reshape(1, DV_A), out_scale, npp=8)
        yr_s, ret_s = _retention_step(rg_s[:, 0:d].reshape(db, HR, dk, 1),
                                      rg_s[:, d:2 * d].reshape(db, HR, dk, 1),
                                      rg_s[:, 2 * d:4 * d].reshape(db, HR, 1, dv),
                                      rg_s[:, 4 * d:6 * d].reshape(db, HR, 1, dv),
                                      state_ret[i], gam.reshape(HR, 1, dv), bet.reshape(HR, 1, dv))
        x1_s = _merge(ya_s.reshape(db, d), yr_s.reshape(db, 2 * d), rg_s, hs, wpa, wpr, wo, g1, b1, alpha, tm=db)
        hs = _ffn(x1_s, p_sample[i].reshape(db, -1), wfi, wfo, wpg, wpp, g2, b2, alpha, tm=db)
        ks_l.append(ks_f.reshape(db, 1, HA, DV_A))
        vs_l.append(vs_f.reshape(db, 1, HA, DV_A))
        rs_l.append(ret_s)

    return (hp.reshape(1, seq, d), hs.reshape(db, 1, d), jnp.stack(kp_l), jnp.stack(vp_l), jnp.stack(rp_l),
            jnp.stack(ks_l), jnp.stack(vs_l), jnp.stack(rs_l))
```

```python
import functools
import math

import jax
import jax.numpy as jnp
from jax import lax
from jax.experimental import pallas as pl
from jax.experimental.pallas import tpu as pltpu

F32 = jnp.float32
BF16 = jnp.bfloat16

HA = 8
DH_A = 64
DV_A = 128
HR = 4
PAGE_SIZE = 128
EPS = 1e-5
NEG = -1e30
LOG2E = 1.4426950408889634
SOFTMAX_ROWS = 64

LANES = 128
V7X_VMEM_BYTES = 64 * 1024 * 1024
VMEM_LIMIT = V7X_VMEM_BYTES - 8 * 1024 * 1024


def _cparams(sem):
    return pltpu.CompilerParams(dimension_semantics=sem, vmem_limit_bytes=VMEM_LIMIT)


def _sigmoid(x):
    return 1.0 / (1.0 + jnp.exp(-x))


def _layer_norm(x, g, b):
    mu = jnp.mean(x, axis=-1, keepdims=True)
    xc = x - mu
    var = jnp.mean(xc * xc, axis=-1, keepdims=True)
    return xc * lax.rsqrt(var + EPS) * g + b


def _dot(a, b):
    return jnp.dot(a, b, preferred_element_type=F32)


def _dot_nt(a, b):
    return lax.dot_general(a, b, (((1,), (1,)), ((), ())), preferred_element_type=F32)


def _row_tile(m, target):
    t = min(m, target)
    assert m % t == 0, (m, t)
    return t


def _qkv_kernel(x_ref, w_ref, *refs, d, q_scale, n_alias):
    q_ref, kf_ref, vf_ref, *flash_refs = refs[n_alias:]
    xb = x_ref[...].astype(BF16)
    q = _dot(xb, w_ref[:, 0:d])
    q_ref[...] = (q * q_scale).astype(q_ref.dtype)
    k = _dot(xb, w_ref[:, d:2 * d])
    kf_ref[...] = k
    v = _dot(xb, w_ref[:, 2 * d:3 * d])
    vf_ref[...] = v
    if flash_refs:
        kb_ref, vt_ref = flash_refs
        kb_ref[...] = k.astype(BF16)
        for h in range(HA):
            vt_ref[h] = v[:, h * DV_A:(h + 1) * DV_A].T.astype(BF16)


def _qkv_proj(x, w_bf, tm, for_flash, layer, depth, kv_prev):
    m, d = x.shape
    tm = _row_tile(m, tm)
    n_alias = 0 if kv_prev is None else 2
    kern = functools.partial(_qkv_kernel, d=d, q_scale=(DH_A ** -0.5) * (LOG2E if for_flash else 1.0),
                             n_alias=n_alias)
    row = lambda i: (i, 0)
    slab = pl.BlockSpec((None, tm, d), lambda i: (layer, i, 0))
    out_specs = [pl.BlockSpec((tm, d), row), slab, slab]
    out_shape = [jax.ShapeDtypeStruct((m, d), BF16 if for_flash else F32),
                 jax.ShapeDtypeStruct((depth, m, d), F32), jax.ShapeDtypeStruct((depth, m, d), F32)]
    if for_flash:
        out_specs += [pl.BlockSpec((tm, d), row), pl.BlockSpec((HA, DV_A, tm), lambda i: (0, 0, i))]
        out_shape += [jax.ShapeDtypeStruct((m, d), BF16), jax.ShapeDtypeStruct((HA, DV_A, m), BF16)]
    in_specs = [pl.BlockSpec((tm, d), row), pl.BlockSpec((d, 3 * d), lambda i: (0, 0))]
    args = [x, w_bf]
    aliases = {}
    if kv_prev is not None:
        in_specs += [pl.BlockSpec(memory_space=pl.ANY)] * 2
        args += list(kv_prev)
        aliases = {2: 1, 3: 2}
    return pl.pallas_call(
        kern,
        grid=(m // tm,),
        in_specs=in_specs,
        out_specs=out_specs,
        out_shape=out_shape,
        input_output_aliases=aliases,
        compiler_params=_cparams(("arbitrary",)),
        name="qkv_proj",
    )(*args)


def _proj_kernel(x_ref, w_ref, s_ref, o_ref, xb_ref):
    @pl.when(pl.program_id(1) == 0)
    def _():
        xb_ref[...] = x_ref[...].astype(BF16)

    o_ref[...] = (_dot(xb_ref[...], w_ref[...]) * s_ref[...]).astype(o_ref.dtype)


def _proj(x, w_bf, col_scale, col0, n, tm, tn, out_dtype=F32):
    m, d = x.shape
    tm = _row_tile(m, tm)
    assert n % tn == 0 and col0 % tn == 0
    jb = col0 // tn
    return pl.pallas_call(
        _proj_kernel,
        grid=(m // tm, n // tn),
        in_specs=[pl.BlockSpec((tm, d), lambda i, j: (i, 0)),
                  pl.BlockSpec((d, tn), lambda i, j: (0, jb + j)),
                  pl.BlockSpec((1, tn), lambda i, j: (0, j))],
        out_specs=pl.BlockSpec((tm, tn), lambda i, j: (i, j)),
        out_shape=jax.ShapeDtypeStruct((m, n), out_dtype),
        scratch_shapes=[pltpu.VMEM((tm, d), BF16)],
        compiler_params=_cparams(("parallel", "arbitrary")),
        name="col_proj",
    )(x, w_bf, col_scale)


def _flash_kernel(slopes_ref, lam_ref, q_ref, k_ref, vt_ref, g_ref, o_ref,
                  bias_ref, st_ref, acc_ref, s_a, s_b, p_a, p_b, al_a, al_b, *, t, out_scale):
    h = pl.program_id(0)
    qi = pl.program_id(1)
    slope2 = slopes_ref[h] * LOG2E

    @pl.when(qi == 0)
    def _():
        jj = lax.broadcasted_iota(jnp.int32, (t, t), 0)
        ii = lax.broadcasted_iota(jnp.int32, (t, t), 1)
        b = (jj - ii).astype(F32) * slope2
        bias_ref[0] = b
        bias_ref[1] = jnp.where(ii >= jj, b, NEG)

    q = q_ref[...]
    lane = lax.broadcasted_iota(jnp.int32, q.shape, 1)
    zero = jnp.zeros_like(q)
    qm = (jnp.where(lane < DH_A, q, zero), jnp.where(lane >= DH_A, q, zero))

    st_ref[0:1, :] = jnp.full((1, t), -jnp.inf, F32)
    st_ref[1:2, :] = jnp.zeros((1, t), F32)
    st_ref[2:3, :] = jnp.full((1, t), -jnp.inf, F32)
    st_ref[3:4, :] = jnp.zeros((1, t), F32)
    acc_ref[...] = jnp.zeros_like(acc_ref)
    p_b[...] = jnp.zeros_like(p_b)
    al_b[...] = jnp.ones_like(al_b)

    def qk(kb, s_dst):
        k_blk = k_ref[pl.ds(pl.multiple_of(kb * t, t), t), :]
        b = bias_ref[(kb == qi).astype(jnp.int32)]
        for c in range(2):
            s_dst[c] = _dot_nt(k_blk, qm[c]) + b

    def pv(kb, p_src, al_src):
        vt_blk = vt_ref[:, pl.ds(pl.multiple_of(kb * t, t), t)]
        for c in range(2):
            acc_ref[c] = al_src[c:c + 1, :] * acc_ref[c] + _dot(vt_blk, p_src[c])

    def softmax(s_src, p_dst, al_dst, cterm):
        ch = SOFTMAX_ROWS
        for c in range(2):
            m_old = st_ref[2 * c:2 * c + 1, :]
            l_old = st_ref[2 * c + 1:2 * c + 2, :]
            mx8 = None
            for r0 in range(0, t, ch):
                cm = jnp.max(s_src[c, r0:r0 + ch, :].reshape(ch // 8, 8, t), axis=0)
                mx8 = cm if mx8 is None else jnp.maximum(mx8, cm)
            m_new = jnp.maximum(m_old, jnp.max(mx8, axis=0, keepdims=True) - cterm)
            alpha = jnp.exp2(m_old - m_new)
            shift = m_new + cterm
            ls8 = jnp.zeros((8, t), F32)
            for r0 in range(0, t, ch):
                pch = jnp.exp2(s_src[c, r0:r0 + ch, :] - shift)
                ls8 = ls8 + jnp.sum(pch.reshape(ch // 8, 8, t), axis=0)
                p_dst[c, r0:r0 + ch, :] = pch.astype(BF16)
            st_ref[2 * c:2 * c + 1, :] = m_new
            st_ref[2 * c + 1:2 * c + 2, :] = alpha * l_old + jnp.sum(ls8, axis=0, keepdims=True)
            al_dst[c:c + 1, :] = alpha

    def stage(kb, s_cur, s_nxt, p_cur, p_prv, al_cur, al_prv):
        qk(kb + 1, s_nxt)
        pv(jnp.maximum(kb - 1, 0), p_prv, al_prv)
        softmax(s_cur, p_cur, al_cur, slope2 * ((qi - kb) * t).astype(F32))

    qk(0, s_a)

    def pair(j, carry):
        stage(2 * j, s_a, s_b, p_a, p_b, al_a, al_b)
        stage(2 * j + 1, s_b, s_a, p_b, p_a, al_b, al_a)
        return carry

    lax.fori_loop(0, qi // 2, pair, 0)

    @pl.when(qi % 2 == 0)
    def _():
        pv(jnp.maximum(qi - 1, 0), p_b, al_b)
        softmax(s_a, p_a, al_a, jnp.float32(0.0))
        pv(qi, p_a, al_a)

    @pl.when(qi % 2 == 1)
    def _():
        stage(qi - 1, s_a, s_b, p_a, p_b, al_a, al_b)
        pv(qi - 1, p_a, al_a)
        softmax(s_b, p_b, al_b, jnp.float32(0.0))
        pv(qi, p_b, al_b)

    lam = lam_ref[0]
    o1 = acc_ref[0] / st_ref[1:2, :]
    o2 = acc_ref[1] / st_ref[3:4, :]
    o = o1 - lam * o2
    ms = jnp.mean(o * o, axis=0, keepdims=True)
    y = o * lax.rsqrt(ms + EPS) * (g_ref[...] * out_scale)
    o_ref[...] = y.T.astype(o_ref.dtype)


def _flash_attention(q_bf, k_bf, vt_bf, slopes, lam, g_col, out_scale, t):
    s, d = q_bf.shape
    t = _row_tile(s, t)
    kern = functools.partial(_flash_kernel, t=t, out_scale=out_scale)
    smem = pl.BlockSpec(memory_space=pltpu.SMEM)
    return pl.pallas_call(
        kern,
        grid=(HA, s // t),
        in_specs=[smem, smem,
                  pl.BlockSpec((t, DV_A), lambda h, i: (i, h)),
                  pl.BlockSpec((s, DV_A), lambda h, i: (0, h)),
                  pl.BlockSpec((None, DV_A, s), lambda h, i: (h, 0, 0)),
                  pl.BlockSpec((DV_A, 1), lambda h, i: (0, 0))],
        out_specs=pl.BlockSpec((t, DV_A), lambda h, i: (i, h)),
        out_shape=jax.ShapeDtypeStruct((s, d), BF16),
        scratch_shapes=[pltpu.VMEM((2, t, t), F32),
                        pltpu.VMEM((8, t), F32), pltpu.VMEM((2, DV_A, t), F32),
                        pltpu.VMEM((2, t, t), F32), pltpu.VMEM((2, t, t), F32),
                        pltpu.VMEM((2, t, t), BF16), pltpu.VMEM((2, t, t), BF16),
                        pltpu.VMEM((8, t), F32), pltpu.VMEM((8, t), F32)],
        compiler_params=_cparams(("arbitrary", "arbitrary")),
        name="diff_flash_attn",
    )(slopes, lam, q_bf, k_bf, vt_bf, g_col)


def _retention_kernel(rc_ref, q_ref, k_ref, v_ref, gr_ref, gam_ref, bet_ref, y_ref, s_ref,
                      decay_ref, cross_ref, kdec_ref, *, L):
    h = pl.program_id(0)
    c = pl.program_id(1)
    lg = rc_ref[0, h]
    g_chunk = rc_ref[1, h]

    @pl.when(c == 0)
    def _():
        ii = lax.broadcasted_iota(jnp.int32, (L, L), 0)
        jj = lax.broadcasted_iota(jnp.int32, (L, L), 1)
        rel = (ii - jj).astype(F32)
        decay_ref[...] = jnp.where(rel >= 0, jnp.exp(lg * jnp.maximum(rel, 0.0)), 0.0)
        idx = lax.broadcasted_iota(jnp.int32, (L, 1), 0).astype(F32)
        cross_ref[...] = jnp.exp(lg * (idx + 1.0))
        kdec_ref[...] = jnp.exp(lg * (L - 1.0 - idx))
        s_ref[...] = jnp.zeros_like(s_ref)

    q = q_ref[...]
    k = k_ref[...]
    vb = v_ref[...].astype(BF16)
    qb = q.astype(BF16)
    scores = _dot_nt(qb, k.astype(BF16)) * decay_ref[...]
    s0 = s_ref[...]
    o = _dot(scores.astype(BF16), vb) + _dot(qb, s0.astype(BF16)) * cross_ref[...]
    kd_t = (k * kdec_ref[...]).T.astype(BF16)
    s_ref[...] = g_chunk * s0 + _dot(kd_t, vb)

    mu = jnp.mean(o, axis=-1, keepdims=True)
    oc = o - mu
    var = jnp.mean(oc * oc, axis=-1, keepdims=True)
    n = oc * lax.rsqrt(var + EPS) * gam_ref[...] + bet_ref[...]
    gr = gr_ref[...]
    y_ref[...] = (n * (gr * _sigmoid(gr))).astype(y_ref.dtype)


def _retention_prompt(rg, ret_consts, gam, bet, L):
    s = rg.shape[0]
    d = rg.shape[1] // 8
    dk = d // HR
    dv = 2 * d // HR
    L = _row_tile(s, L)
    kern = functools.partial(_retention_kernel, L=L)
    return pl.pallas_call(
        kern,
        grid=(HR, s // L),
        in_specs=[pl.BlockSpec(memory_space=pltpu.SMEM),
                  pl.BlockSpec((L, dk), lambda h, c: (c, h)),
                  pl.BlockSpec((L, dk), lambda h, c: (c, HR + h)),
                  pl.BlockSpec((L, dv), lambda h, c: (c, HR + h)),
                  pl.BlockSpec((L, dv), lambda h, c: (c, 2 * HR + h)),
                  pl.BlockSpec((1, dv), lambda h, c: (0, h)),
                  pl.BlockSpec((1, dv), lambda h, c: (0, h))],
        out_specs=[pl.BlockSpec((L, dv), lambda h, c: (c, h)),
                   pl.BlockSpec((None, dk, dv), lambda h, c: (h, 0, 0))],
        out_shape=[jax.ShapeDtypeStruct((s, 2 * d), BF16),
                   jax.ShapeDtypeStruct((HR, dk, dv), F32)],
        scratch_shapes=[pltpu.VMEM((L, L), F32), pltpu.VMEM((L, 1), F32), pltpu.VMEM((L, 1), F32)],
        compiler_params=_cparams(("arbitrary", "arbitrary")),
        name="retention_scan",
    )(ret_consts, rg, rg, rg, rg, gam, bet)


def _retention_step_kernel(q_ref, k_ref, v_ref, gr_ref, s0_ref, gam_ref, bet_ref, *refs, n_alias):
    y_ref, s1_ref = refs[n_alias:]
    for h in range(HR):
        g = 1.0 - 2.0 ** (-5.0 - h)
        q = q_ref[h]
        k = k_ref[h]
        v = v_ref[h]
        s0 = s0_ref[h]
        qk = jnp.sum(q * k, axis=0, keepdims=True)
        o = qk * v + jnp.sum(q * s0, axis=0, keepdims=True) * g
        s1_ref[h] = g * s0 + k * v
        mu = jnp.mean(o, axis=-1, keepdims=True)
        oc = o - mu
        var = jnp.mean(oc * oc, axis=-1, keepdims=True)
        n = oc * lax.rsqrt(var + EPS) * gam_ref[h] + bet_ref[h]
        gr = gr_ref[h]
        y_ref[h] = (n * (gr * _sigmoid(gr))).astype(y_ref.dtype)


def _retention_step(q_col, k_col, v_row, gr_row, state_all, gam, bet, layer, s_prev):
    depth, b, _, dk, dv = state_all.shape
    per_b = lambda i: (i, 0, 0, 0)
    slab = pl.BlockSpec((None, None, HR, dk, dv), lambda i: (layer, i, 0, 0, 0))
    n_alias = 0 if s_prev is None else 1
    in_specs = [pl.BlockSpec((None, HR, dk, 1), per_b),
                pl.BlockSpec((None, HR, dk, 1), per_b),
                pl.BlockSpec((None, HR, 1, dv), per_b),
                pl.BlockSpec((None, HR, 1, dv), per_b),
                slab,
                pl.BlockSpec((HR, 1, dv), lambda i: (0, 0, 0)),
                pl.BlockSpec((HR, 1, dv), lambda i: (0, 0, 0))]
    args = [q_col, k_col, v_row, gr_row, state_all, gam, bet]
    aliases = {}
    if s_prev is not None:
        in_specs.append(pl.BlockSpec(memory_space=pl.ANY))
        args.append(s_prev)
        aliases = {7: 1}
    return pl.pallas_call(
        functools.partial(_retention_step_kernel, n_alias=n_alias),
        grid=(b,),
        in_specs=in_specs,
        out_specs=[pl.BlockSpec((None, HR, 1, dv), per_b), slab],
        out_shape=[jax.ShapeDtypeStruct((b, HR, 1, dv), BF16),
                   jax.ShapeDtypeStruct((depth, b, HR, dk, dv), F32)],
        input_output_aliases=aliases,
        compiler_params=_cparams(("arbitrary",)),
        name="retention_step",
    )(*args)


def _decode_kernel(pt_ref, lam_ref, qcol_ref, knew_ref, vnew_ref, g_ref, *rest,
                   npp, n_steps, past, out_scale):
    k_refs = rest[:npp]
    v_refs = rest[npp:2 * npp]
    o_ref = rest[2 * npp]
    wq_ref, e_ref, sc_ref, a_ref, acc_ref, self_ref = rest[2 * npp + 1:]
    s = pl.program_id(1)
    d = HA * DV_A
    blk = npp * PAGE_SIZE

    @pl.when(s == 0)
    def _():
        r = lax.broadcasted_iota(jnp.int32, (d, LANES), 0)
        n = lax.broadcasted_iota(jnp.int32, (d, LANES), 1)
        wq = jnp.where(n == r // DH_A, qcol_ref[...], 0.0).astype(BF16)
        wq_ref[...] = wq
        n2 = lax.broadcasted_iota(jnp.int32, (LANES, d), 0)
        c2 = lax.broadcasted_iota(jnp.int32, (LANES, d), 1)
        e_ref[...] = jnp.where(n2 == 2 * (c2 // DV_A), 1.0, 0.0).astype(BF16)
        kn = jnp.broadcast_to(knew_ref[...], (8, d)).astype(BF16)
        self_ref[...] = _dot(kn, wq)

    @pl.when(s < n_steps)
    def _():
        for r in range(npp):
            sc = None
            for h in range(HA):
                kh = k_refs[r][pl.ds(h, PAGE_SIZE, stride=HA), :].astype(BF16)
                part = _dot(kh, wq_ref[h * DV_A:(h + 1) * DV_A, :])
                sc = part if sc is None else sc + part
            row0 = pl.multiple_of(s * blk + r * PAGE_SIZE, PAGE_SIZE)
            sc_ref[pl.ds(row0, PAGE_SIZE), :] = sc

    @pl.when(s == n_steps)
    def _():
        lane = lax.broadcasted_iota(jnp.int32, (1, LANES), 1)
        slope = jnp.exp2(((lane // 2) + 1).astype(F32) * (-8.0 / HA))
        lam = lam_ref[0]
        ch = min(512, past)
        nch = past // ch

        def bias(i):
            t = lax.broadcasted_iota(jnp.int32, (ch, LANES), 0) + i * ch
            return (t - past).astype(F32) * slope

        def max_body(i, m):
            r0 = pl.multiple_of(i * ch, ch)
            sv = sc_ref[pl.ds(r0, ch), :] + bias(i)
            sc_ref[pl.ds(r0, ch), :] = sv
            return jnp.maximum(m, jnp.max(sv, axis=0, keepdims=True))

        s_self = self_ref[0:1, :]
        m = lax.fori_loop(0, nch, max_body, s_self)

        def sum_body(i, l):
            r0 = pl.multiple_of(i * ch, ch)
            p = jnp.exp(sc_ref[pl.ds(r0, ch), :] - m)
            sc_ref[pl.ds(r0, ch), :] = p
            return l + jnp.sum(p, axis=0, keepdims=True)

        p_self = jnp.exp(s_self - m)
        l = lax.fori_loop(0, nch, sum_body, p_self)
        inv = 1.0 / l

        def comb_body(i, carry):
            r0 = pl.multiple_of(i * ch, ch)
            pn = sc_ref[pl.ds(r0, ch), :] * inv
            a = pn - lam * pltpu.roll(pn, LANES - 1, 1)
            a_ref[pl.ds(r0, ch), :] = a.astype(BF16)
            return carry

        lax.fori_loop(0, nch, comb_body, 0)
        pn_self = p_self * inv
        a_self = pn_self - lam * pltpu.roll(pn_self, LANES - 1, 1)
        a_self_x = _dot(jnp.broadcast_to(a_self, (8, LANES)).astype(BF16), e_ref[...])
        acc_ref[...] = jnp.zeros_like(acc_ref)
        acc_ref[0:1, :] = a_self_x[0:1, :] * vnew_ref[...]

    @pl.when(s >= n_steps)
    def _():
        for h in range(HA):
            acc = acc_ref[:, h * DV_A:(h + 1) * DV_A]
            for r in range(npp):
                row0 = pl.multiple_of((s - n_steps) * blk + r * PAGE_SIZE, PAGE_SIZE)
                a_x = _dot(a_ref[pl.ds(row0, PAGE_SIZE), :], e_ref[:, h * DV_A:(h + 1) * DV_A])
                prod = a_x * v_refs[r][pl.ds(h, PAGE_SIZE, stride=HA), :]
                acc = acc + jnp.sum(prod.reshape(PAGE_SIZE // 8, 8, DV_A), axis=0)
            acc_ref[:, h * DV_A:(h + 1) * DV_A] = acc

    @pl.when(s == 2 * n_steps - 1)
    def _():
        o = jnp.sum(acc_ref[...], axis=0, keepdims=True)
        g = g_ref[...] * out_scale
        for h in range(HA):
            oh = o[:, h * DV_A:(h + 1) * DV_A]
            ms = jnp.mean(oh * oh, axis=-1, keepdims=True)
            o_ref[:, h * DV_A:(h + 1) * DV_A] = (oh * lax.rsqrt(ms + EPS) * g).astype(o_ref.dtype)


def _decode_attention(layer, cache_k, cache_v, page_table, lam, q_col, k_new, v_new, g_row, out_scale, npp):
    b, n_pages = page_table.shape
    d = HA * DV_A
    assert n_pages % npp == 0
    n_steps = n_pages // npp
    past = n_pages * PAGE_SIZE
    kern = functools.partial(_decode_kernel, npp=npp, n_steps=n_steps, past=past, out_scale=out_scale)

    def k_map(r):
        return lambda bi, s, pt: (layer, pt[bi, jnp.minimum(s, n_steps - 1) * npp + r], 0, 0)

    def v_map(r):
        return lambda bi, s, pt: (layer, pt[bi, jnp.maximum(s - n_steps, 0) * npp + r], 0, 0)

    page_block = (None, None, PAGE_SIZE * HA, DV_A)
    per_b = lambda bi, s, pt: (bi, 0, 0)
    grid_spec = pltpu.PrefetchScalarGridSpec(
        num_scalar_prefetch=1,
        grid=(b, 2 * n_steps),
        in_specs=[pl.BlockSpec(memory_space=pltpu.SMEM),
                  pl.BlockSpec((None, d, 1), per_b),
                  pl.BlockSpec((None, 1, d), per_b),
                  pl.BlockSpec((None, 1, d), per_b),
                  pl.BlockSpec((1, DV_A), lambda bi, s, pt: (0, 0))]
                 + [pl.BlockSpec(page_block, k_map(r)) for r in range(npp)]
                 + [pl.BlockSpec(page_block, v_map(r)) for r in range(npp)],
        out_specs=pl.BlockSpec((None, 1, d), per_b),
        scratch_shapes=[pltpu.VMEM((d, LANES), BF16), pltpu.VMEM((LANES, d), BF16),
                        pltpu.VMEM((past, LANES), F32), pltpu.VMEM((past, LANES), BF16),
                        pltpu.VMEM((8, d), F32), pltpu.VMEM((8, LANES), F32)],
    )
    return pl.pallas_call(
        kern,
        grid_spec=grid_spec,
        out_shape=jax.ShapeDtypeStruct((b, 1, d), BF16),
        compiler_params=_cparams(("arbitrary", "arbitrary")),
        name="paged_decode_attn",
    )(page_table, lam, q_col, k_new, v_new, g_row, *([cache_k] * npp), *([cache_v] * npp))


def _merge_kernel(ya_ref, yr_ref, ga_ref, gb_ref, h_ref, wpa_ref, wpr_ref, wo_ref, g1_ref, b1_ref, o_ref, *, alpha):
    ya = _dot(ya_ref[...], wpa_ref[...])
    yr = _dot(yr_ref[...], wpr_ref[...])
    mix = (_sigmoid(ga_ref[...]) * ya + _sigmoid(gb_ref[...]) * yr).astype(BF16)
    x = alpha * h_ref[...] + _dot(mix, wo_ref[...])
    o_ref[...] = _layer_norm(x, g1_ref[...], b1_ref[...])


def _merge(y_a, y_r, rg, h, wpa, wpr, wo, g1, b1, alpha, tm):
    m, d = h.shape
    tm = _row_tile(m, tm)
    row = lambda i: (i, 0)
    const = lambda i: (0, 0)
    kern = functools.partial(_merge_kernel, alpha=alpha)
    return pl.pallas_call(
        kern,
        grid=(m // tm,),
        in_specs=[pl.BlockSpec((tm, d), row),
                  pl.BlockSpec((tm, 2 * d), row),
                  pl.BlockSpec((tm, d), lambda i: (i, 6)),
                  pl.BlockSpec((tm, d), lambda i: (i, 7)),
                  pl.BlockSpec((tm, d), row),
                  pl.BlockSpec((d, d), const),
                  pl.BlockSpec((2 * d, d), const),
                  pl.BlockSpec((d, d), const),
                  pl.BlockSpec((1, d), const),
                  pl.BlockSpec((1, d), const)],
        out_specs=pl.BlockSpec((tm, d), row),
        out_shape=jax.ShapeDtypeStruct((m, d), F32),
        compiler_params=_cparams(("parallel",)),
        name="branch_merge_ln1",
    )(y_a, y_r, rg, rg, h, wpa, wpr, wo, g1, b1)


def _ffn_kernel(x_ref, p_ref, win_ref, wout_ref, wg_ref, wp_ref, g2_ref, b2_ref, o_ref, *, alpha, dff, chunk):
    x = x_ref[...]
    xb = x.astype(BF16)
    acc = jnp.zeros(x.shape, F32)
    for c0 in range(0, dff, chunk):
        u = _dot(xb, win_ref[:, c0:c0 + chunk])
        g = _dot(xb, win_ref[:, dff + c0:dff + c0 + chunk])
        act = (u * _sigmoid(u) * g).astype(BF16)
        acc = acc + _dot(act, wout_ref[c0:c0 + chunk, :])
    x2 = _layer_norm(alpha * x + acc, g2_ref[...], b2_ref[...])
    gate = _sigmoid(_dot(x2.astype(BF16), wg_ref[...]))
    o_ref[...] = x2 + gate * _dot(p_ref[...].astype(BF16), wp_ref[...])


def _ffn(x, p_all, layer, win, wout, wg, wp, g2, b2, alpha, tm):
    m, d = x.shape
    dff = wout.shape[0]
    dp = p_all.shape[-1]
    tm = _row_tile(m, tm)
    chunk = dff // 2
    assert chunk % LANES == 0
    row = lambda i: (i, 0)
    const = lambda i: (0, 0)
    once = pl.Buffered(1)
    kern = functools.partial(_ffn_kernel, alpha=alpha, dff=dff, chunk=chunk)
    return pl.pallas_call(
        kern,
        grid=(m // tm,),
        in_specs=[pl.BlockSpec((tm, d), row),
                  pl.BlockSpec((None, tm, dp), lambda i: (layer, i, 0)),
                  pl.BlockSpec((d, 2 * dff), const, pipeline_mode=once),
                  pl.BlockSpec((dff, d), const, pipeline_mode=once),
                  pl.BlockSpec((d, d), const, pipeline_mode=once),
                  pl.BlockSpec((dp, d), const, pipeline_mode=once),
                  pl.BlockSpec((1, d), const),
                  pl.BlockSpec((1, d), const)],
        out_specs=pl.BlockSpec((tm, d), row),
        out_shape=jax.ShapeDtypeStruct((m, d), F32),
        compiler_params=_cparams(("parallel",)),
        name="ffn_ln2_ple",
    )(x, p_all, win, wout, wg, wp, g2, b2)


def kernel(x_prompt, x_sample, cache_k, cache_v, state_ret, page_table, p_prompt, p_sample, w_in, lambda_q1, lambda_k1, lambda_q2, lambda_k2, attn_norm_g, ret_norm_g, ret_norm_b, w_pa, w_pr, w_o, ln1_g, ln1_b, ln2_g, ln2_b, w_ffn_in, w_ffn_out, w_ple_gate, w_ple_proj):
    depth = w_in.shape[0]
    bp, seq, d = x_prompt.shape
    db, dseq, _ = x_sample.shape
    assert bp == 1 and dseq == 1 and d == HA * DV_A
    dk = d // HR
    dv = 2 * d // HR
    alpha = (2 * depth) ** 0.25
    n_phys = cache_k.shape[1]

    slopes = 2.0 ** (-8.0 * (jnp.arange(HA, dtype=F32) + 1.0) / HA)
    log_g = jnp.log(1.0 - 2.0 ** (-5.0 - jnp.arange(HR, dtype=F32)))
    ret_chunk = min(seq, 256)
    ret_consts = jnp.stack([log_g, jnp.exp(log_g * ret_chunk)])
    rg_scale = jnp.concatenate([jnp.ones((d,), F32), jnp.full((d,), dk ** -0.5, F32),
                                jnp.ones((6 * d,), F32)]).reshape(1, 8 * d)
    cache_k2 = cache_k.reshape(depth, n_phys, PAGE_SIZE * HA, DV_A)
    cache_v2 = cache_v.reshape(depth, n_phys, PAGE_SIZE * HA, DV_A)
    page_table = page_table.astype(jnp.int32)

    hp = x_prompt.reshape(seq, d)
    hs = x_sample.reshape(db, d)
    rp_l = []
    kvp = kvs = rs_all = None
    pp_all = p_prompt.reshape(depth, seq, -1)
    ps_all = p_sample.reshape(depth, db, -1)
    for i in range(depth):
        lam_init = 0.8 - 0.6 * math.exp(-0.3 * i)
        lam = (jnp.exp(jnp.sum(lambda_q1[i].astype(F32) * lambda_k1[i].astype(F32)))
               - jnp.exp(jnp.sum(lambda_q2[i].astype(F32) * lambda_k2[i].astype(F32)))
               + lam_init).reshape(1)
        out_scale = 1.0 - lam_init
        w_in_bf = w_in[i].astype(BF16)
        wpa, wpr, wo = w_pa[i].astype(BF16), w_pr[i].astype(BF16), w_o[i].astype(BF16)
        wfi, wfo = w_ffn_in[i].astype(BF16), w_ffn_out[i].astype(BF16)
        wpg, wpp = w_ple_gate[i].astype(BF16), w_ple_proj[i].astype(BF16)
        g1, b1 = ln1_g[i].reshape(1, d), ln1_b[i].reshape(1, d)
        g2, b2 = ln2_g[i].reshape(1, d), ln2_b[i].reshape(1, d)
        gam, bet = ret_norm_g[i].reshape(1, 2 * d), ret_norm_b[i].reshape(1, 2 * d)
        attn_g = attn_norm_g[i]

        q_bf, kp_all, vp_all, k_bf, vt_bf = _qkv_proj(hp, w_in_bf, 512, True, i, depth, kvp)
        kvp = (kp_all, vp_all)
        rg = _proj(hp, w_in_bf, rg_scale, 3 * d, 8 * d, tm=1024, tn=1024)
        y_a = _flash_attention(q_bf, k_bf, vt_bf, slopes, lam, attn_g.reshape(DV_A, 1), out_scale, t=512)
        y_r, ret_p = _retention_prompt(rg, ret_consts, gam, bet, L=ret_chunk)
        x1 = _merge(y_a, y_r, rg, hp, wpa, wpr, wo, g1, b1, alpha, tm=512)
        hp = _ffn(x1, pp_all, i, wfi, wfo, wpg, wpp, g2, b2, alpha, tm=512)
        rp_l.append(ret_p.reshape(1, HR, dk, dv))

        q_s, ks_all, vs_all = _qkv_proj(hs, w_in_bf, db, False, i, depth, kvs)
        kvs = (ks_all, vs_all)
        rg_s = _proj(hs, w_in_bf, rg_scale, 3 * d, 8 * d, tm=db, tn=1024)
        ya_s = _decode_attention(i, cache_k2, cache_v2, page_table, lam, q_s.reshape(db, d, 1),
                                 ks_all[i].reshape(db, 1, d), vs_all[i].reshape(db, 1, d),
                                 attn_g.reshape(1, DV_A), out_scale, npp=8)
        yr_s, rs_all = _retention_step(rg_s[:, 0:d].reshape(db, HR, dk, 1),
                                       rg_s[:, d:2 * d].reshape(db, HR, dk, 1),
                                       rg_s[:, 2 * d:4 * d].reshape(db, HR, 1, dv),
                                       rg_s[:, 4 * d:6 * d].reshape(db, HR, 1, dv),
                                       state_ret, gam.reshape(HR, 1, dv), bet.reshape(HR, 1, dv), i, rs_all)
        x1_s = _merge(ya_s.reshape(db, d), yr_s.reshape(db, 2 * d), rg_s, hs, wpa, wpr, wo, g1, b1, alpha, tm=db)
        hs = _ffn(x1_s, ps_all, i, wfi, wfo, wpg, wpp, g2, b2, alpha, tm=db)

    return (hp.reshape(1, seq, d), hs.reshape(db, 1, d),
            kvp[0].reshape(depth, 1, seq, HA, DV_A), kvp[1].reshape(depth, 1, seq, HA, DV_A), jnp.stack(rp_l),
            kvs[0].reshape(depth, db, 1, HA, DV_A), kvs[1].reshape(depth, db, 1, HA, DV_A), rs_all)
```

```python
import functools
import math

import jax
import jax.numpy as jnp
from jax import lax
from jax.experimental import pallas as pl
from jax.experimental.pallas import tpu as pltpu

F32 = jnp.float32
BF16 = jnp.bfloat16

HA = 8
DH_A = 64
DV_A = 128
HR = 4
PAGE_SIZE = 128
EPS = 1e-5
NEG = -1e30
LOG2E = 1.4426950408889634
SOFTMAX_ROWS = 64

LANES = 128
V7X_VMEM_BYTES = 64 * 1024 * 1024
VMEM_LIMIT = V7X_VMEM_BYTES - 8 * 1024 * 1024


def _cparams(sem):
    return pltpu.CompilerParams(dimension_semantics=sem, vmem_limit_bytes=VMEM_LIMIT)


def _sigmoid(x):
    return 1.0 / (1.0 + jnp.exp(-x))


def _layer_norm(x, g, b):
    mu = jnp.mean(x, axis=-1, keepdims=True)
    xc = x - mu
    var = jnp.mean(xc * xc, axis=-1, keepdims=True)
    return xc * lax.rsqrt(var + EPS) * g + b


def _dot(a, b):
    return jnp.dot(a, b, preferred_element_type=F32)


def _dot_nt(a, b):
    return lax.dot_general(a, b, (((1,), (1,)), ((), ())), preferred_element_type=F32)


def _row_tile(m, target):
    t = min(m, target)
    assert m % t == 0, (m, t)
    return t


def _qkv_kernel(x_ref, w_ref, *refs, d, q_scale, n_alias):
    q_ref, kf_ref, vf_ref, *flash_refs = refs[n_alias:]
    xb = x_ref[...].astype(BF16)
    q = _dot(xb, w_ref[:, 0:d])
    q_ref[...] = (q * q_scale).astype(q_ref.dtype)
    k = _dot(xb, w_ref[:, d:2 * d])
    kf_ref[...] = k
    v = _dot(xb, w_ref[:, 2 * d:3 * d])
    vf_ref[...] = v
    if flash_refs:
        kb_ref, vt_ref = flash_refs
        kb_ref[...] = k.astype(BF16)
        for h in range(HA):
            vt_ref[h] = v[:, h * DV_A:(h + 1) * DV_A].T.astype(BF16)


def _qkv_proj(x, w_bf, tm, for_flash, layer, depth, kv_prev):
    m, d = x.shape
    tm = _row_tile(m, tm)
    n_alias = 0 if kv_prev is None else 2
    kern = functools.partial(_qkv_kernel, d=d, q_scale=(DH_A ** -0.5) * (LOG2E if for_flash else 1.0),
                             n_alias=n_alias)
    row = lambda i: (i, 0)
    slab = pl.BlockSpec((None, tm, d), lambda i: (layer, i, 0))
    out_specs = [pl.BlockSpec((tm, d), row), slab, slab]
    out_shape = [jax.ShapeDtypeStruct((m, d), BF16 if for_flash else F32),
                 jax.ShapeDtypeStruct((depth, m, d), F32), jax.ShapeDtypeStruct((depth, m, d), F32)]
    if for_flash:
        out_specs += [pl.BlockSpec((tm, d), row), pl.BlockSpec((HA, DV_A, tm), lambda i: (0, 0, i))]
        out_shape += [jax.ShapeDtypeStruct((m, d), BF16), jax.ShapeDtypeStruct((HA, DV_A, m), BF16)]
    in_specs = [pl.BlockSpec((tm, d), row), pl.BlockSpec((d, 3 * d), lambda i: (0, 0))]
    args = [x, w_bf]
    aliases = {}
    if kv_prev is not None:
        in_specs += [pl.BlockSpec(memory_space=pl.ANY)] * 2
        args += list(kv_prev)
        aliases = {2: 1, 3: 2}
    return pl.pallas_call(
        kern,
        grid=(m // tm,),
        in_specs=in_specs,
        out_specs=out_specs,
        out_shape=out_shape,
        input_output_aliases=aliases,
        compiler_params=_cparams(("arbitrary",)),
        name="qkv_proj",
    )(*args)


def _proj_kernel(x_ref, w_ref, s_ref, o_ref, xb_ref):
    @pl.when(pl.program_id(1) == 0)
    def _():
        xb_ref[...] = x_ref[...].astype(BF16)

    o_ref[...] = (_dot(xb_ref[...], w_ref[...]) * s_ref[...]).astype(o_ref.dtype)


def _proj(x, w_bf, col_scale, col0, n, tm, tn, out_dtype=F32):
    m, d = x.shape
    tm = _row_tile(m, tm)
    assert n % tn == 0 and col0 % tn == 0
    jb = col0 // tn
    return pl.pallas_call(
        _proj_kernel,
        grid=(m // tm, n // tn),
        in_specs=[pl.BlockSpec((tm, d), lambda i, j: (i, 0)),
                  pl.BlockSpec((d, tn), lambda i, j: (0, jb + j)),
                  pl.BlockSpec((1, tn), lambda i, j: (0, j))],
        out_specs=pl.BlockSpec((tm, tn), lambda i, j: (i, j)),
        out_shape=jax.ShapeDtypeStruct((m, n), out_dtype),
        scratch_shapes=[pltpu.VMEM((tm, d), BF16)],
        compiler_params=_cparams(("parallel", "arbitrary")),
        name="col_proj",
    )(x, w_bf, col_scale)


def _flash_kernel(slopes_ref, lam_ref, q_ref, k_ref, vt_ref, g_ref, o_ref,
                  bias_ref, st_ref, acc_ref, s_a, s_b, p_a, p_b, al_a, al_b, *, t, out_scale):
    h = pl.program_id(0)
    qi = pl.program_id(1)
    slope2 = slopes_ref[h] * LOG2E

    @pl.when(qi == 0)
    def _():
        jj = lax.broadcasted_iota(jnp.int32, (t, t), 0)
        ii = lax.broadcasted_iota(jnp.int32, (t, t), 1)
        b = (jj - ii).astype(F32) * slope2
        bias_ref[0] = b
        bias_ref[1] = jnp.where(ii >= jj, b, NEG)

    q = q_ref[...]
    lane = lax.broadcasted_iota(jnp.int32, q.shape, 1)
    zero = jnp.zeros_like(q)
    qm = (jnp.where(lane < DH_A, q, zero), jnp.where(lane >= DH_A, q, zero))

    st_ref[0:1, :] = jnp.full((1, t), -jnp.inf, F32)
    st_ref[1:2, :] = jnp.zeros((1, t), F32)
    st_ref[2:3, :] = jnp.full((1, t), -jnp.inf, F32)
    st_ref[3:4, :] = jnp.zeros((1, t), F32)
    acc_ref[...] = jnp.zeros_like(acc_ref)
    p_b[...] = jnp.zeros_like(p_b)
    al_b[...] = jnp.ones_like(al_b)

    def qk(kb, s_dst):
        k_blk = k_ref[pl.ds(pl.multiple_of(kb * t, t), t), :]
        b = bias_ref[(kb == qi).astype(jnp.int32)]
        for c in range(2):
            s_dst[c] = _dot_nt(k_blk, qm[c]) + b

    def pv(kb, p_src, al_src):
        vt_blk = vt_ref[:, pl.ds(pl.multiple_of(kb * t, t), t)]
        for c in range(2):
            acc_ref[c] = al_src[c:c + 1, :] * acc_ref[c] + _dot(vt_blk, p_src[c])

    def softmax(s_src, p_dst, al_dst, cterm):
        ch = SOFTMAX_ROWS
        for c in range(2):
            m_old = st_ref[2 * c:2 * c + 1, :]
            l_old = st_ref[2 * c + 1:2 * c + 2, :]
            mx8 = None
            for r0 in range(0, t, ch):
                cm = jnp.max(s_src[c, r0:r0 + ch, :].reshape(ch // 8, 8, t), axis=0)
                mx8 = cm if mx8 is None else jnp.maximum(mx8, cm)
            m_new = jnp.maximum(m_old, jnp.max(mx8, axis=0, keepdims=True) - cterm)
            alpha = jnp.exp2(m_old - m_new)
            shift = m_new + cterm
            ls8 = jnp.zeros((8, t), F32)
            for r0 in range(0, t, ch):
                pch = jnp.exp2(s_src[c, r0:r0 + ch, :] - shift)
                ls8 = ls8 + jnp.sum(pch.reshape(ch // 8, 8, t), axis=0)
                p_dst[c, r0:r0 + ch, :] = pch.astype(BF16)
            st_ref[2 * c:2 * c + 1, :] = m_new
            st_ref[2 * c + 1:2 * c + 2, :] = alpha * l_old + jnp.sum(ls8, axis=0, keepdims=True)
            al_dst[c:c + 1, :] = alpha

    def stage(kb, s_cur, s_nxt, p_cur, p_prv, al_cur, al_prv):
        qk(kb + 1, s_nxt)
        pv(jnp.maximum(kb - 1, 0), p_prv, al_prv)
        softmax(s_cur, p_cur, al_cur, slope2 * ((qi - kb) * t).astype(F32))

    qk(0, s_a)

    def pair(j, carry):
        stage(2 * j, s_a, s_b, p_a, p_b, al_a, al_b)
        stage(2 * j + 1, s_b, s_a, p_b, p_a, al_b, al_a)
        return carry

    lax.fori_loop(0, qi // 2, pair, 0)

    @pl.when(qi % 2 == 0)
    def _():
        pv(jnp.maximum(qi - 1, 0), p_b, al_b)
        softmax(s_a, p_a, al_a, jnp.float32(0.0))
        pv(qi, p_a, al_a)

    @pl.when(qi % 2 == 1)
    def _():
        stage(qi - 1, s_a, s_b, p_a, p_b, al_a, al_b)
        pv(qi - 1, p_a, al_a)
        softmax(s_b, p_b, al_b, jnp.float32(0.0))
        pv(qi, p_b, al_b)

    lam = lam_ref[0]
    o1 = acc_ref[0] / st_ref[1:2, :]
    o2 = acc_ref[1] / st_ref[3:4, :]
    o = o1 - lam * o2
    ms = jnp.mean(o * o, axis=0, keepdims=True)
    y = o * lax.rsqrt(ms + EPS) * (g_ref[...] * out_scale)
    o_ref[...] = y.T.astype(o_ref.dtype)


def _flash_attention(q_bf, k_bf, vt_bf, slopes, lam, g_col, out_scale, t):
    s, d = q_bf.shape
    t = _row_tile(s, t)
    kern = functools.partial(_flash_kernel, t=t, out_scale=out_scale)
    smem = pl.BlockSpec(memory_space=pltpu.SMEM)
    return pl.pallas_call(
        kern,
        grid=(HA, s // t),
        in_specs=[smem, smem,
                  pl.BlockSpec((t, DV_A), lambda h, i: (i, h)),
                  pl.BlockSpec((s, DV_A), lambda h, i: (0, h), pipeline_mode=pl.Buffered(1)),
                  pl.BlockSpec((None, DV_A, s), lambda h, i: (h, 0, 0), pipeline_mode=pl.Buffered(1)),
                  pl.BlockSpec((DV_A, 1), lambda h, i: (0, 0))],
        out_specs=pl.BlockSpec((t, DV_A), lambda h, i: (i, h)),
        out_shape=jax.ShapeDtypeStruct((s, d), BF16),
        scratch_shapes=[pltpu.VMEM((2, t, t), F32),
                        pltpu.VMEM((8, t), F32), pltpu.VMEM((2, DV_A, t), F32),
                        pltpu.VMEM((2, t, t), F32), pltpu.VMEM((2, t, t), F32),
                        pltpu.VMEM((2, t, t), BF16), pltpu.VMEM((2, t, t), BF16),
                        pltpu.VMEM((8, t), F32), pltpu.VMEM((8, t), F32)],
        compiler_params=_cparams(("arbitrary", "arbitrary")),
        name="diff_flash_attn",
    )(slopes, lam, q_bf, k_bf, vt_bf, g_col)


def _retention_kernel(rc_ref, q_ref, k_ref, v_ref, gr_ref, gam_ref, bet_ref, y_ref, s_ref,
                      decay_ref, cross_ref, kdec_ref, *, L):
    h = pl.program_id(0)
    c = pl.program_id(1)
    lg = rc_ref[0, h]
    g_chunk = rc_ref[1, h]

    @pl.when(c == 0)
    def _():
        ii = lax.broadcasted_iota(jnp.int32, (L, L), 0)
        jj = lax.broadcasted_iota(jnp.int32, (L, L), 1)
        rel = (ii - jj).astype(F32)
        decay_ref[...] = jnp.where(rel >= 0, jnp.exp(lg * jnp.maximum(rel, 0.0)), 0.0)
        idx = lax.broadcasted_iota(jnp.int32, (L, 1), 0).astype(F32)
        cross_ref[...] = jnp.exp(lg * (idx + 1.0))
        kdec_ref[...] = jnp.exp(lg * (L - 1.0 - idx))
        s_ref[...] = jnp.zeros_like(s_ref)

    q = q_ref[...]
    k = k_ref[...]
    vb = v_ref[...].astype(BF16)
    qb = q.astype(BF16)
    scores = _dot_nt(qb, k.astype(BF16)) * decay_ref[...]
    s0 = s_ref[...]
    o = _dot(scores.astype(BF16), vb) + _dot(qb, s0.astype(BF16)) * cross_ref[...]
    kd_t = (k * kdec_ref[...]).T.astype(BF16)
    s_ref[...] = g_chunk * s0 + _dot(kd_t, vb)

    mu = jnp.mean(o, axis=-1, keepdims=True)
    oc = o - mu
    var = jnp.mean(oc * oc, axis=-1, keepdims=True)
    n = oc * lax.rsqrt(var + EPS) * gam_ref[...] + bet_ref[...]
    gr = gr_ref[...]
    y_ref[...] = (n * (gr * _sigmoid(gr))).astype(y_ref.dtype)


def _retention_prompt(rg, ret_consts, gam, bet, L):
    s = rg.shape[0]
    d = rg.shape[1] // 8
    dk = d // HR
    dv = 2 * d // HR
    L = _row_tile(s, L)
    kern = functools.partial(_retention_kernel, L=L)
    return pl.pallas_call(
        kern,
        grid=(HR, s // L),
        in_specs=[pl.BlockSpec(memory_space=pltpu.SMEM),
                  pl.BlockSpec((L, dk), lambda h, c: (c, h)),
                  pl.BlockSpec((L, dk), lambda h, c: (c, HR + h)),
                  pl.BlockSpec((L, dv), lambda h, c: (c, HR + h)),
                  pl.BlockSpec((L, dv), lambda h, c: (c, 2 * HR + h)),
                  pl.BlockSpec((1, dv), lambda h, c: (0, h)),
                  pl.BlockSpec((1, dv), lambda h, c: (0, h))],
        out_specs=[pl.BlockSpec((L, dv), lambda h, c: (c, h)),
                   pl.BlockSpec((None, dk, dv), lambda h, c: (h, 0, 0))],
        out_shape=[jax.ShapeDtypeStruct((s, 2 * d), BF16),
                   jax.ShapeDtypeStruct((HR, dk, dv), F32)],
        scratch_shapes=[pltpu.VMEM((L, L), F32), pltpu.VMEM((L, 1), F32), pltpu.VMEM((L, 1), F32)],
        compiler_params=_cparams(("arbitrary", "arbitrary")),
        name="retention_scan",
    )(ret_consts, rg, rg, rg, rg, gam, bet)


def _retention_step_kernel(q_ref, k_ref, v_ref, gr_ref, s0_ref, gam_ref, bet_ref, *refs, n_alias):
    y_ref, s1_ref = refs[n_alias:]
    for h in range(HR):
        g = 1.0 - 2.0 ** (-5.0 - h)
        q = q_ref[h]
        k = k_ref[h]
        v = v_ref[h]
        s0 = s0_ref[h]
        qk = jnp.sum(q * k, axis=0, keepdims=True)
        o = qk * v + jnp.sum(q * s0, axis=0, keepdims=True) * g
        s1_ref[h] = g * s0 + k * v
        mu = jnp.mean(o, axis=-1, keepdims=True)
        oc = o - mu
        var = jnp.mean(oc * oc, axis=-1, keepdims=True)
        n = oc * lax.rsqrt(var + EPS) * gam_ref[h] + bet_ref[h]
        gr = gr_ref[h]
        y_ref[h] = (n * (gr * _sigmoid(gr))).astype(y_ref.dtype)


def _retention_step(q_col, k_col, v_row, gr_row, state_all, gam, bet, layer, s_prev):
    depth, b, _, dk, dv = state_all.shape
    per_b = lambda i: (i, 0, 0, 0)
    slab = pl.BlockSpec((None, None, HR, dk, dv), lambda i: (layer, i, 0, 0, 0))
    n_alias = 0 if s_prev is None else 1
    in_specs = [pl.BlockSpec((None, HR, dk, 1), per_b),
                pl.BlockSpec((None, HR, dk, 1), per_b),
                pl.BlockSpec((None, HR, 1, dv), per_b),
                pl.BlockSpec((None, HR, 1, dv), per_b),
                slab,
                pl.BlockSpec((HR, 1, dv), lambda i: (0, 0, 0)),
                pl.BlockSpec((HR, 1, dv), lambda i: (0, 0, 0))]
    args = [q_col, k_col, v_row, gr_row, state_all, gam, bet]
    aliases = {}
    if s_prev is not None:
        in_specs.append(pl.BlockSpec(memory_space=pl.ANY))
        args.append(s_prev)
        aliases = {7: 1}
    return pl.pallas_call(
        functools.partial(_retention_step_kernel, n_alias=n_alias),
        grid=(b,),
        in_specs=in_specs,
        out_specs=[pl.BlockSpec((None, HR, 1, dv), per_b), slab],
        out_shape=[jax.ShapeDtypeStruct((b, HR, 1, dv), BF16),
                   jax.ShapeDtypeStruct((depth, b, HR, dk, dv), F32)],
        input_output_aliases=aliases,
        compiler_params=_cparams(("arbitrary",)),
        name="retention_step",
    )(*args)


def _decode_kernel(pt_ref, lam_ref, qcol_ref, knew_ref, vnew_ref, g_ref, *rest,
                   npp, n_steps, past, out_scale):
    k_refs = rest[:npp]
    v_refs = rest[npp:2 * npp]
    o_ref = rest[2 * npp]
    wq_ref, e_ref, sc_ref, b0_ref, acc_ref, st_ref = rest[2 * npp + 1:]
    s = pl.program_id(1)
    d = HA * DV_A
    blk = npp * PAGE_SIZE
    lane = lax.broadcasted_iota(jnp.int32, (1, LANES), 1)
    slope = jnp.exp2(((lane // 2) + 1).astype(F32) * (-8.0 / HA))

    @pl.when(s == 0)
    def _():
        r = lax.broadcasted_iota(jnp.int32, (d, LANES), 0)
        n = lax.broadcasted_iota(jnp.int32, (d, LANES), 1)
        wq = jnp.where(n == r // DH_A, qcol_ref[...], 0.0).astype(BF16)
        wq_ref[...] = wq
        n2 = lax.broadcasted_iota(jnp.int32, (LANES, d), 0)
        c2 = lax.broadcasted_iota(jnp.int32, (LANES, d), 1)
        e_ref[...] = jnp.where(n2 == 2 * (c2 // DV_A), 1.0, 0.0).astype(BF16)
        kn = jnp.broadcast_to(knew_ref[...], (8, d)).astype(BF16)
        s_self = _dot(kn, wq)[0:1, :]
        st_ref[0:1, :] = s_self
        st_ref[1:2, :] = s_self
        t = lax.broadcasted_iota(jnp.int32, (PAGE_SIZE, LANES), 0)
        b0_ref[...] = (t - past).astype(F32) * slope

    @pl.when(s < n_steps)
    def _():
        m = st_ref[1:2, :]
        for r in range(npp):
            sc = None
            for h in range(HA):
                kh = k_refs[r][pl.ds(h, PAGE_SIZE, stride=HA), :].astype(BF16)
                part = _dot(kh, wq_ref[h * DV_A:(h + 1) * DV_A, :])
                sc = part if sc is None else sc + part
            row0 = pl.multiple_of(s * blk + r * PAGE_SIZE, PAGE_SIZE)
            sc = sc + b0_ref[...] + row0.astype(F32) * slope
            sc_ref[pl.ds(row0, PAGE_SIZE), :] = sc
            m = jnp.maximum(m, jnp.max(sc, axis=0, keepdims=True))
        st_ref[1:2, :] = m

    @pl.when(s == n_steps)
    def _():
        m = st_ref[1:2, :]
        ch = min(512, past)

        def sum_body(i, l):
            r0 = pl.multiple_of(i * ch, ch)
            p = jnp.exp(sc_ref[pl.ds(r0, ch), :] - m)
            sc_ref[pl.ds(r0, ch), :] = p
            return l + jnp.sum(p, axis=0, keepdims=True)

        p_self = jnp.exp(st_ref[0:1, :] - m)
        l = lax.fori_loop(0, past // ch, sum_body, p_self)
        inv = 1.0 / l
        st_ref[2:3, :] = inv
        pn_self = p_self * inv
        a_self = pn_self - lam_ref[0] * pltpu.roll(pn_self, LANES - 1, 1)
        a_self_x = _dot(jnp.broadcast_to(a_self, (8, LANES)).astype(BF16), e_ref[...])
        acc_ref[...] = jnp.zeros_like(acc_ref)
        acc_ref[0:1, :] = a_self_x[0:1, :] * vnew_ref[...]

    @pl.when(s >= n_steps)
    def _():
        inv = st_ref[2:3, :]
        lam = lam_ref[0]
        acc = [acc_ref[:, h * DV_A:(h + 1) * DV_A] for h in range(HA)]
        for r in range(npp):
            row0 = pl.multiple_of((s - n_steps) * blk + r * PAGE_SIZE, PAGE_SIZE)
            pn = sc_ref[pl.ds(row0, PAGE_SIZE), :] * inv
            a = (pn - lam * pltpu.roll(pn, LANES - 1, 1)).astype(BF16)
            for h in range(HA):
                a_x = _dot(a, e_ref[:, h * DV_A:(h + 1) * DV_A])
                prod = a_x * v_refs[r][pl.ds(h, PAGE_SIZE, stride=HA), :]
                acc[h] = acc[h] + jnp.sum(prod.reshape(PAGE_SIZE // 8, 8, DV_A), axis=0)
        for h in range(HA):
            acc_ref[:, h * DV_A:(h + 1) * DV_A] = acc[h]

    @pl.when(s == 2 * n_steps - 1)
    def _():
        o = jnp.sum(acc_ref[...], axis=0, keepdims=True)
        g = g_ref[...] * out_scale
        for h in range(HA):
            oh = o[:, h * DV_A:(h + 1) * DV_A]
            ms = jnp.mean(oh * oh, axis=-1, keepdims=True)
            o_ref[:, h * DV_A:(h + 1) * DV_A] = (oh * lax.rsqrt(ms + EPS) * g).astype(o_ref.dtype)


def _decode_attention(layer, cache_k, cache_v, page_table, lam, q_col, k_new, v_new, g_row, out_scale, npp):
    b, n_pages = page_table.shape
    d = HA * DV_A
    assert n_pages % npp == 0
    n_steps = n_pages // npp
    past = n_pages * PAGE_SIZE
    kern = functools.partial(_decode_kernel, npp=npp, n_steps=n_steps, past=past, out_scale=out_scale)

    def k_map(r):
        return lambda bi, s, pt: (layer, pt[bi, jnp.minimum(s, n_steps - 1) * npp + r], 0, 0)

    def v_map(r):
        return lambda bi, s, pt: (layer, pt[bi, jnp.maximum(s - n_steps, 0) * npp + r], 0, 0)

    page_block = (None, None, PAGE_SIZE * HA, DV_A)
    per_b = lambda bi, s, pt: (bi, 0, 0)
    grid_spec = pltpu.PrefetchScalarGridSpec(
        num_scalar_prefetch=1,
        grid=(b, 2 * n_steps),
        in_specs=[pl.BlockSpec(memory_space=pltpu.SMEM),
                  pl.BlockSpec((None, d, 1), per_b),
                  pl.BlockSpec((None, 1, d), per_b),
                  pl.BlockSpec((None, 1, d), per_b),
                  pl.BlockSpec((1, DV_A), lambda bi, s, pt: (0, 0))]
                 + [pl.BlockSpec(page_block, k_map(r)) for r in range(npp)]
                 + [pl.BlockSpec(page_block, v_map(r)) for r in range(npp)],
        out_specs=pl.BlockSpec((None, 1, d), per_b),
        scratch_shapes=[pltpu.VMEM((d, LANES), BF16), pltpu.VMEM((LANES, d), BF16),
                        pltpu.VMEM((past, LANES), F32), pltpu.VMEM((PAGE_SIZE, LANES), F32),
                        pltpu.VMEM((8, d), F32), pltpu.VMEM((8, LANES), F32)],
    )
    return pl.pallas_call(
        kern,
        grid_spec=grid_spec,
        out_shape=jax.ShapeDtypeStruct((b, 1, d), BF16),
        compiler_params=_cparams(("arbitrary", "arbitrary")),
        name="paged_decode_attn",
    )(page_table, lam, q_col, k_new, v_new, g_row, *([cache_k] * npp), *([cache_v] * npp))


def _merge_kernel(ya_ref, yr_ref, ga_ref, gb_ref, h_ref, wpa_ref, wpr_ref, wo_ref, g1_ref, b1_ref, o_ref, *, alpha):
    ya = _dot(ya_ref[...], wpa_ref[...])
    yr = _dot(yr_ref[...], wpr_ref[...])
    mix = (_sigmoid(ga_ref[...]) * ya + _sigmoid(gb_ref[...]) * yr).astype(BF16)
    x = alpha * h_ref[...] + _dot(mix, wo_ref[...])
    o_ref[...] = _layer_norm(x, g1_ref[...], b1_ref[...])


def _merge(y_a, y_r, rg, h, wpa, wpr, wo, g1, b1, alpha, tm):
    m, d = h.shape
    tm = _row_tile(m, tm)
    row = lambda i: (i, 0)
    const = lambda i: (0, 0)
    kern = functools.partial(_merge_kernel, alpha=alpha)
    return pl.pallas_call(
        kern,
        grid=(m // tm,),
        in_specs=[pl.BlockSpec((tm, d), row),
                  pl.BlockSpec((tm, 2 * d), row),
                  pl.BlockSpec((tm, d), lambda i: (i, 6)),
                  pl.BlockSpec((tm, d), lambda i: (i, 7)),
                  pl.BlockSpec((tm, d), row),
                  pl.BlockSpec((d, d), const),
                  pl.BlockSpec((2 * d, d), const),
                  pl.BlockSpec((d, d), const),
                  pl.BlockSpec((1, d), const),
                  pl.BlockSpec((1, d), const)],
        out_specs=pl.BlockSpec((tm, d), row),
        out_shape=jax.ShapeDtypeStruct((m, d), F32),
        compiler_params=_cparams(("parallel",)),
        name="branch_merge_ln1",
    )(y_a, y_r, rg, rg, h, wpa, wpr, wo, g1, b1)


def _ffn_kernel(x_ref, p_ref, win_ref, wout_ref, wg_ref, wp_ref, g2_ref, b2_ref, o_ref, *, alpha, dff, chunk):
    x = x_ref[...]
    xb = x.astype(BF16)
    acc = jnp.zeros(x.shape, F32)
    for c0 in range(0, dff, chunk):
        u = _dot(xb, win_ref[:, c0:c0 + chunk])
        g = _dot(xb, win_ref[:, dff + c0:dff + c0 + chunk])
        act = (u * _sigmoid(u) * g).astype(BF16)
        acc = acc + _dot(act, wout_ref[c0:c0 + chunk, :])
    x2 = _layer_norm(alpha * x + acc, g2_ref[...], b2_ref[...])
    gate = _sigmoid(_dot(x2.astype(BF16), wg_ref[...]))
    o_ref[...] = x2 + gate * _dot(p_ref[...].astype(BF16), wp_ref[...])


def _ffn(x, p_all, layer, win, wout, wg, wp, g2, b2, alpha, tm):
    m, d = x.shape
    dff = wout.shape[0]
    dp = p_all.shape[-1]
    tm = _row_tile(m, tm)
    chunk = dff // 2
    assert chunk % LANES == 0
    row = lambda i: (i, 0)
    const = lambda i: (0, 0)
    once = pl.Buffered(1)
    kern = functools.partial(_ffn_kernel, alpha=alpha, dff=dff, chunk=chunk)
    return pl.pallas_call(
        kern,
        grid=(m // tm,),
        in_specs=[pl.BlockSpec((tm, d), row),
                  pl.BlockSpec((None, tm, dp), lambda i: (layer, i, 0)),
                  pl.BlockSpec((d, 2 * dff), const, pipeline_mode=once),
                  pl.BlockSpec((dff, d), const, pipeline_mode=once),
                  pl.BlockSpec((d, d), const, pipeline_mode=once),
                  pl.BlockSpec((dp, d), const, pipeline_mode=once),
                  pl.BlockSpec((1, d), const),
                  pl.BlockSpec((1, d), const)],
        out_specs=pl.BlockSpec((tm, d), row),
        out_shape=jax.ShapeDtypeStruct((m, d), F32),
        compiler_params=_cparams(("parallel",)),
        name="ffn_ln2_ple",
    )(x, p_all, win, wout, wg, wp, g2, b2)


def kernel(x_prompt, x_sample, cache_k, cache_v, state_ret, page_table, p_prompt, p_sample, w_in, lambda_q1, lambda_k1, lambda_q2, lambda_k2, attn_norm_g, ret_norm_g, ret_norm_b, w_pa, w_pr, w_o, ln1_g, ln1_b, ln2_g, ln2_b, w_ffn_in, w_ffn_out, w_ple_gate, w_ple_proj):
    depth = w_in.shape[0]
    bp, seq, d = x_prompt.shape
    db, dseq, _ = x_sample.shape
    assert bp == 1 and dseq == 1 and d == HA * DV_A
    dk = d // HR
    dv = 2 * d // HR
    alpha = (2 * depth) ** 0.25
    n_phys = cache_k.shape[1]

    slopes = 2.0 ** (-8.0 * (jnp.arange(HA, dtype=F32) + 1.0) / HA)
    log_g = jnp.log(1.0 - 2.0 ** (-5.0 - jnp.arange(HR, dtype=F32)))
    ret_chunk = min(seq, 256)
    ret_consts = jnp.stack([log_g, jnp.exp(log_g * ret_chunk)])
    rg_scale = jnp.concatenate([jnp.ones((d,), F32), jnp.full((d,), dk ** -0.5, F32),
                                jnp.ones((6 * d,), F32)]).reshape(1, 8 * d)
    cache_k2 = cache_k.reshape(depth, n_phys, PAGE_SIZE * HA, DV_A)
    cache_v2 = cache_v.reshape(depth, n_phys, PAGE_SIZE * HA, DV_A)
    page_table = page_table.astype(jnp.int32)

    hp = x_prompt.reshape(seq, d)
    hs = x_sample.reshape(db, d)
    rp_l = []
    kvp = kvs = rs_all = None
    pp_all = p_prompt.reshape(depth, seq, -1)
    ps_all = p_sample.reshape(depth, db, -1)
    for i in range(depth):
        lam_init = 0.8 - 0.6 * math.exp(-0.3 * i)
        lam = (jnp.exp(jnp.sum(lambda_q1[i].astype(F32) * lambda_k1[i].astype(F32)))
               - jnp.exp(jnp.sum(lambda_q2[i].astype(F32) * lambda_k2[i].astype(F32)))
               + lam_init).reshape(1)
        out_scale = 1.0 - lam_init
        w_in_bf = w_in[i].astype(BF16)
        wpa, wpr, wo = w_pa[i].astype(BF16), w_pr[i].astype(BF16), w_o[i].astype(BF16)
        wfi, wfo = w_ffn_in[i].astype(BF16), w_ffn_out[i].astype(BF16)
        wpg, wpp = w_ple_gate[i].astype(BF16), w_ple_proj[i].astype(BF16)
        g1, b1 = ln1_g[i].reshape(1, d), ln1_b[i].reshape(1, d)
        g2, b2 = ln2_g[i].reshape(1, d), ln2_b[i].reshape(1, d)
        gam, bet = ret_norm_g[i].reshape(1, 2 * d), ret_norm_b[i].reshape(1, 2 * d)
        attn_g = attn_norm_g[i]

        q_bf, kp_all, vp_all, k_bf, vt_bf = _qkv_proj(hp, w_in_bf, 512, True, i, depth, kvp)
        kvp = (kp_all, vp_all)
        rg = _proj(hp, w_in_bf, rg_scale, 3 * d, 8 * d, tm=1024, tn=1024)
        y_a = _flash_attention(q_bf, k_bf, vt_bf, slopes, lam, attn_g.reshape(DV_A, 1), out_scale, t=1024)
        y_r, ret_p = _retention_prompt(rg, ret_consts, gam, bet, L=ret_chunk)
        x1 = _merge(y_a, y_r, rg, hp, wpa, wpr, wo, g1, b1, alpha, tm=512)
        hp = _ffn(x1, pp_all, i, wfi, wfo, wpg, wpp, g2, b2, alpha, tm=512)
        rp_l.append(ret_p.reshape(1, HR, dk, dv))

        q_s, ks_all, vs_all = _qkv_proj(hs, w_in_bf, db, False, i, depth, kvs)
        kvs = (ks_all, vs_all)
        rg_s = _proj(hs, w_in_bf, rg_scale, 3 * d, 8 * d, tm=db, tn=1024)
        ya_s = _decode_attention(i, cache_k2, cache_v2, page_table, lam, q_s.reshape(db, d, 1),
                                 ks_all[i].reshape(db, 1, d), vs_all[i].reshape(db, 1, d),
                                 attn_g.reshape(1, DV_A), out_scale, npp=min(16, page_table.shape[1]))
        yr_s, rs_all = _retention_step(rg_s[:, 0:d].reshape(db, HR, dk, 1),
                                       rg_s[:, d:2 * d].reshape(db, HR, dk, 1),
                                       rg_s[:, 2 * d:4 * d].reshape(db, HR, 1, dv),
                                       rg_s[:, 4 * d:6 * d].reshape(db, HR, 1, dv),
                                       state_ret, gam.reshape(HR, 1, dv), bet.reshape(HR, 1, dv), i, rs_all)
        x1_s = _merge(ya_s.reshape(db, d), yr_s.reshape(db, 2 * d), rg_s, hs, wpa, wpr, wo, g1, b1, alpha, tm=db)
        hs = _ffn(x1_s, ps_all, i, wfi, wfo, wpg, wpp, g2, b2, alpha, tm=db)

    return (hp.reshape(1, seq, d), hs.reshape(db, 1, d),
            kvp[0].reshape(depth, 1, seq, HA, DV_A), kvp[1].reshape(depth, 1, seq, HA, DV_A), jnp.stack(rp_l),
            kvs[0].reshape(depth, db, 1, HA, DV_A), kvs[1].reshape(depth, db, 1, HA, DV_A), rs_all)
```

```python
import functools
import math

import jax
import jax.numpy as jnp
import numpy as np
from jax import lax
from jax.experimental import pallas as pl
from jax.experimental.pallas import tpu as pltpu

F32 = jnp.float32
BF16 = jnp.bfloat16

HA = 8
DH_A = 64
DV_A = 128
HR = 4
PAGE_SIZE = 128
EPS = 1e-5
NEG = -1e30
LOG2E = 1.4426950408889634
SOFTMAX_ROWS = 64

LANES = 128
V7X_VMEM_BYTES = 64 * 1024 * 1024
VMEM_LIMIT = V7X_VMEM_BYTES - 8 * 1024 * 1024


def _cparams(sem):
    return pltpu.CompilerParams(dimension_semantics=sem, vmem_limit_bytes=VMEM_LIMIT)


def _sigmoid(x):
    return 1.0 / (1.0 + jnp.exp(-x))


def _layer_norm(x, g, b):
    mu = jnp.mean(x, axis=-1, keepdims=True)
    xc = x - mu
    var = jnp.mean(xc * xc, axis=-1, keepdims=True)
    return xc * lax.rsqrt(var + EPS) * g + b


def _dot(a, b):
    return jnp.dot(a, b, preferred_element_type=F32)


def _dot_nt(a, b):
    return lax.dot_general(a, b, (((1,), (1,)), ((), ())), preferred_element_type=F32)


def _row_tile(m, target):
    t = min(m, target)
    assert m % t == 0, (m, t)
    return t


def _qkv_kernel(x_ref, w_ref, *refs, d, q_scale, n_alias, feat_block):
    if feat_block:
        qfeat_ref = refs[0]
        refs = refs[1:]
    q_ref, kf_ref, vf_ref, *flash_refs = refs[n_alias:]
    xb = x_ref[...].astype(BF16)
    q = _dot(xb, w_ref[:, 0:d]) * q_scale
    k = _dot(xb, w_ref[:, d:2 * d])
    kf_ref[...] = k
    v = _dot(xb, w_ref[:, 2 * d:3 * d])
    vf_ref[...] = v
    if not flash_refs:
        q_ref[...] = q.astype(q_ref.dtype)
        return
    kb_ref, vt_ref = flash_refs
    tm = q.shape[0]
    lane = lax.broadcasted_iota(jnp.int32, (tm, LANES), 1)
    first_half = lane < DH_A
    j = (lax.broadcasted_iota(jnp.int32, (tm, LANES), 0) + pl.program_id(0) * tm) % feat_block
    kfeat = jnp.where(lane < DH_A + 3, j // 16, j % 16).astype(F32)
    kfeat = jnp.where((lane >= DH_A) & (lane < DH_A + 6), kfeat, 0.0)
    for h in range(HA):
        qh = q[:, h * DV_A:(h + 1) * DV_A]
        kh = k[:, h * DV_A:(h + 1) * DV_A]
        qf = qfeat_ref[h:h + 1, :]
        c0 = 2 * h * LANES
        q_ref[:, c0:c0 + LANES] = jnp.where(first_half, qh, qf).astype(BF16)
        q_ref[:, c0 + LANES:c0 + 2 * LANES] = jnp.where(first_half, pltpu.roll(qh, DH_A, 1), qf).astype(BF16)
        kb_ref[:, c0:c0 + LANES] = jnp.where(first_half, kh, kfeat).astype(BF16)
        kb_ref[:, c0 + LANES:c0 + 2 * LANES] = jnp.where(first_half, pltpu.roll(kh, DH_A, 1), kfeat).astype(BF16)
        vt_ref[h] = v[:, h * DV_A:(h + 1) * DV_A].T.astype(BF16)


def _qkv_proj(x, w_bf, tm, layer, depth, kv_prev, qfeat=None, feat_block=0):
    m, d = x.shape
    tm = _row_tile(m, tm)
    for_flash = qfeat is not None
    n_alias = 0 if kv_prev is None else 2
    kern = functools.partial(_qkv_kernel, d=d, q_scale=(DH_A ** -0.5) * (LOG2E if for_flash else 1.0),
                             n_alias=n_alias, feat_block=feat_block)
    row = lambda i: (i, 0)
    slab = pl.BlockSpec((None, tm, d), lambda i: (layer, i, 0))
    dq = 2 * d if for_flash else d
    out_specs = [pl.BlockSpec((tm, dq), row), slab, slab]
    out_shape = [jax.ShapeDtypeStruct((m, dq), BF16 if for_flash else F32),
                 jax.ShapeDtypeStruct((depth, m, d), F32), jax.ShapeDtypeStruct((depth, m, d), F32)]
    in_specs = [pl.BlockSpec((tm, d), row), pl.BlockSpec((d, 3 * d), lambda i: (0, 0))]
    args = [x, w_bf]
    if for_flash:
        assert feat_block % tm == 0 or tm % feat_block == 0
        out_specs += [pl.BlockSpec((tm, dq), row), pl.BlockSpec((HA, DV_A, tm), lambda i: (0, 0, i))]
        out_shape += [jax.ShapeDtypeStruct((m, dq), BF16), jax.ShapeDtypeStruct((HA, DV_A, m), BF16)]
        in_specs.append(pl.BlockSpec((HA, LANES), lambda i: (0, 0)))
        args.append(qfeat)
    n_in = len(args)
    aliases = {}
    if kv_prev is not None:
        in_specs += [pl.BlockSpec(memory_space=pl.ANY)] * 2
        args += list(kv_prev)
        aliases = {n_in: 1, n_in + 1: 2}
    return pl.pallas_call(
        kern,
        grid=(m // tm,),
        in_specs=in_specs,
        out_specs=out_specs,
        out_shape=out_shape,
        input_output_aliases=aliases,
        compiler_params=_cparams(("arbitrary",)),
        name="qkv_proj",
    )(*args)


def _proj_kernel(x_ref, w_ref, s_ref, o_ref, xb_ref):
    @pl.when(pl.program_id(1) == 0)
    def _():
        xb_ref[...] = x_ref[...].astype(BF16)

    o_ref[...] = (_dot(xb_ref[...], w_ref[...]) * s_ref[...]).astype(o_ref.dtype)


def _proj(x, w_bf, col_scale, col0, n, tm, tn, out_dtype=F32):
    m, d = x.shape
    tm = _row_tile(m, tm)
    assert n % tn == 0 and col0 % tn == 0
    jb = col0 // tn
    return pl.pallas_call(
        _proj_kernel,
        grid=(m // tm, n // tn),
        in_specs=[pl.BlockSpec((tm, d), lambda i, j: (i, 0)),
                  pl.BlockSpec((d, tn), lambda i, j: (0, jb + j)),
                  pl.BlockSpec((1, tn), lambda i, j: (0, j))],
        out_specs=pl.BlockSpec((tm, tn), lambda i, j: (i, j)),
        out_shape=jax.ShapeDtypeStruct((m, n), out_dtype),
        scratch_shapes=[pltpu.VMEM((tm, d), BF16)],
        compiler_params=_cparams(("parallel", "arbitrary")),
        name="col_proj",
    )(x, w_bf, col_scale)


def _flash_kernel(slopes_ref, lam_ref, q_ref, k_ref, vt_ref, g_ref, o_ref,
                  bias_ref, st_ref, acc_ref, s_a, s_b, p_a, p_b, al_a, al_b, *, t, out_scale):
    h = pl.program_id(0)
    qi = pl.program_id(1)
    slope2 = slopes_ref[h] * LOG2E

    @pl.when(qi == 0)
    def _():
        jj = lax.broadcasted_iota(jnp.int32, (t, t), 0)
        ii = lax.broadcasted_iota(jnp.int32, (t, t), 1)
        bias_ref[0] = jnp.where(ii >= jj, 0.0, NEG)

    qm = (q_ref[:, 0:LANES], q_ref[:, LANES:2 * LANES])

    st_ref[0:1, :] = jnp.full((1, t), -jnp.inf, F32)
    st_ref[1:2, :] = jnp.zeros((1, t), F32)
    st_ref[2:3, :] = jnp.full((1, t), -jnp.inf, F32)
    st_ref[3:4, :] = jnp.zeros((1, t), F32)
    acc_ref[...] = jnp.zeros_like(acc_ref)
    p_b[...] = jnp.zeros_like(p_b)
    al_b[...] = jnp.ones_like(al_b)

    def qk(kb, s_dst):
        k0 = pl.multiple_of(kb * t, t)
        for c in range(2):
            s_dst[c] = _dot_nt(k_ref[pl.ds(k0, t), c * LANES:(c + 1) * LANES], qm[c])

    def pv(kb, p_src, al_src):
        vt_blk = vt_ref[:, pl.ds(pl.multiple_of(kb * t, t), t)]
        for c in range(2):
            acc_ref[c] = al_src[c:c + 1, :] * acc_ref[c] + _dot(vt_blk, p_src[c])

    def softmax(s_src, p_dst, al_dst, cterm, masked=False):
        ch = SOFTMAX_ROWS

        def rows(c, r0):
            sv = s_src[c, r0:r0 + ch, :]
            return sv + bias_ref[0, r0:r0 + ch, :] if masked else sv

        for c in range(2):
            m_old = st_ref[2 * c:2 * c + 1, :]
            l_old = st_ref[2 * c + 1:2 * c + 2, :]
            mx8 = None
            for r0 in range(0, t, ch):
                cm = jnp.max(rows(c, r0).reshape(ch // 8, 8, t), axis=0)
                mx8 = cm if mx8 is None else jnp.maximum(mx8, cm)
            m_new = jnp.maximum(m_old, jnp.max(mx8, axis=0, keepdims=True) - cterm)
            alpha = jnp.exp2(m_old - m_new)
            shift = m_new + cterm
            ls8 = jnp.zeros((8, t), F32)
            for r0 in range(0, t, ch):
                pch = jnp.exp2(rows(c, r0) - shift)
                ls8 = ls8 + jnp.sum(pch.reshape(ch // 8, 8, t), axis=0)
                p_dst[c, r0:r0 + ch, :] = pch.astype(BF16)
            st_ref[2 * c:2 * c + 1, :] = m_new
            st_ref[2 * c + 1:2 * c + 2, :] = alpha * l_old + jnp.sum(ls8, axis=0, keepdims=True)
            al_dst[c:c + 1, :] = alpha

    def stage(kb, s_cur, s_nxt, p_cur, p_prv, al_cur, al_prv):
        qk(kb + 1, s_nxt)
        pv(jnp.maximum(kb - 1, 0), p_prv, al_prv)
        softmax(s_cur, p_cur, al_cur, slope2 * ((qi - kb) * t).astype(F32))

    qk(0, s_a)

    def pair(j, carry):
        stage(2 * j, s_a, s_b, p_a, p_b, al_a, al_b)
        stage(2 * j + 1, s_b, s_a, p_b, p_a, al_b, al_a)
        return carry

    lax.fori_loop(0, qi // 2, pair, 0)

    @pl.when(qi % 2 == 0)
    def _():
        pv(jnp.maximum(qi - 1, 0), p_b, al_b)
        softmax(s_a, p_a, al_a, jnp.float32(0.0), masked=True)
        pv(qi, p_a, al_a)

    @pl.when(qi % 2 == 1)
    def _():
        stage(qi - 1, s_a, s_b, p_a, p_b, al_a, al_b)
        pv(qi - 1, p_a, al_a)
        softmax(s_b, p_b, al_b, jnp.float32(0.0), masked=True)
        pv(qi, p_b, al_b)

    lam = lam_ref[0]
    o1 = acc_ref[0] / st_ref[1:2, :]
    o2 = acc_ref[1] / st_ref[3:4, :]
    o = o1 - lam * o2
    ms = jnp.mean(o * o, axis=0, keepdims=True)
    y = o * lax.rsqrt(ms + EPS) * (g_ref[...] * out_scale)
    o_ref[...] = y.T.astype(o_ref.dtype)


def _flash_attention(q_bf, k_bf, vt_bf, slopes, lam, g_col, out_scale, t):
    s = q_bf.shape[0]
    d = HA * DV_A
    t = _row_tile(s, t)
    kern = functools.partial(_flash_kernel, t=t, out_scale=out_scale)
    smem = pl.BlockSpec(memory_space=pltpu.SMEM)
    return pl.pallas_call(
        kern,
        grid=(HA, s // t),
        in_specs=[smem, smem,
                  pl.BlockSpec((t, 2 * LANES), lambda h, i: (i, h)),
                  pl.BlockSpec((s, 2 * LANES), lambda h, i: (0, h), pipeline_mode=pl.Buffered(1)),
                  pl.BlockSpec((None, DV_A, s), lambda h, i: (h, 0, 0), pipeline_mode=pl.Buffered(1)),
                  pl.BlockSpec((DV_A, 1), lambda h, i: (0, 0))],
        out_specs=pl.BlockSpec((t, DV_A), lambda h, i: (i, h)),
        out_shape=jax.ShapeDtypeStruct((s, d), BF16),
        scratch_shapes=[pltpu.VMEM((1, t, t), F32),
                        pltpu.VMEM((8, t), F32), pltpu.VMEM((2, DV_A, t), F32),
                        pltpu.VMEM((2, t, t), F32), pltpu.VMEM((2, t, t), F32),
                        pltpu.VMEM((2, t, t), BF16), pltpu.VMEM((2, t, t), BF16),
                        pltpu.VMEM((8, t), F32), pltpu.VMEM((8, t), F32)],
        compiler_params=_cparams(("arbitrary", "arbitrary")),
        name="diff_flash_attn",
    )(slopes, lam, q_bf, k_bf, vt_bf, g_col)


def _retention_kernel(rc_ref, q_ref, k_ref, v_ref, gr_ref, gam_ref, bet_ref, y_ref, s_ref,
                      decay_ref, cross_ref, kdec_ref, *, L):
    h = pl.program_id(0)
    c = pl.program_id(1)
    lg = rc_ref[0, h]
    g_chunk = rc_ref[1, h]

    @pl.when(c == 0)
    def _():
        ii = lax.broadcasted_iota(jnp.int32, (L, L), 0)
        jj = lax.broadcasted_iota(jnp.int32, (L, L), 1)
        rel = (ii - jj).astype(F32)
        decay_ref[...] = jnp.where(rel >= 0, jnp.exp(lg * jnp.maximum(rel, 0.0)), 0.0)
        idx = lax.broadcasted_iota(jnp.int32, (L, 1), 0).astype(F32)
        cross_ref[...] = jnp.exp(lg * (idx + 1.0))
        kdec_ref[...] = jnp.exp(lg * (L - 1.0 - idx))
        s_ref[...] = jnp.zeros_like(s_ref)

    q = q_ref[...]
    k = k_ref[...]
    vb = v_ref[...].astype(BF16)
    qb = q.astype(BF16)
    scores = _dot_nt(qb, k.astype(BF16)) * decay_ref[...]
    s0 = s_ref[...]
    o = _dot(scores.astype(BF16), vb) + _dot(qb, s0.astype(BF16)) * cross_ref[...]
    kd_t = (k * kdec_ref[...]).T.astype(BF16)
    s_ref[...] = g_chunk * s0 + _dot(kd_t, vb)

    mu = jnp.mean(o, axis=-1, keepdims=True)
    oc = o - mu
    var = jnp.mean(oc * oc, axis=-1, keepdims=True)
    n = oc * lax.rsqrt(var + EPS) * gam_ref[...] + bet_ref[...]
    gr = gr_ref[...]
    y_ref[...] = (n * (gr * _sigmoid(gr))).astype(y_ref.dtype)


def _retention_prompt(rg, ret_consts, gam, bet, L):
    s = rg.shape[0]
    d = rg.shape[1] // 8
    dk = d // HR
    dv = 2 * d // HR
    L = _row_tile(s, L)
    kern = functools.partial(_retention_kernel, L=L)
    return pl.pallas_call(
        kern,
        grid=(HR, s // L),
        in_specs=[pl.BlockSpec(memory_space=pltpu.SMEM),
                  pl.BlockSpec((L, dk), lambda h, c: (c, h)),
                  pl.BlockSpec((L, dk), lambda h, c: (c, HR + h)),
                  pl.BlockSpec((L, dv), lambda h, c: (c, HR + h)),
                  pl.BlockSpec((L, dv), lambda h, c: (c, 2 * HR + h)),
                  pl.BlockSpec((1, dv), lambda h, c: (0, h)),
                  pl.BlockSpec((1, dv), lambda h, c: (0, h))],
        out_specs=[pl.BlockSpec((L, dv), lambda h, c: (c, h)),
                   pl.BlockSpec((None, dk, dv), lambda h, c: (h, 0, 0))],
        out_shape=[jax.ShapeDtypeStruct((s, 2 * d), BF16),
                   jax.ShapeDtypeStruct((HR, dk, dv), F32)],
        scratch_shapes=[pltpu.VMEM((L, L), F32), pltpu.VMEM((L, 1), F32), pltpu.VMEM((L, 1), F32)],
        compiler_params=_cparams(("arbitrary", "arbitrary")),
        name="retention_scan",
    )(ret_consts, rg, rg, rg, rg, gam, bet)


def _retention_step_kernel(q_ref, k_ref, v_ref, gr_ref, s0_ref, gam_ref, bet_ref, *refs, n_alias):
    y_ref, s1_ref = refs[n_alias:]
    for h in range(HR):
        g = 1.0 - 2.0 ** (-5.0 - h)
        q = q_ref[h]
        k = k_ref[h]
        v = v_ref[h]
        s0 = s0_ref[h]
        qk = jnp.sum(q * k, axis=0, keepdims=True)
        o = qk * v + jnp.sum(q * s0, axis=0, keepdims=True) * g
        s1_ref[h] = g * s0 + k * v
        mu = jnp.mean(o, axis=-1, keepdims=True)
        oc = o - mu
        var = jnp.mean(oc * oc, axis=-1, keepdims=True)
        n = oc * lax.rsqrt(var + EPS) * gam_ref[h] + bet_ref[h]
        gr = gr_ref[h]
        y_ref[h] = (n * (gr * _sigmoid(gr))).astype(y_ref.dtype)


def _retention_step(q_col, k_col, v_row, gr_row, state_all, gam, bet, layer, s_prev):
    depth, b, _, dk, dv = state_all.shape
    per_b = lambda i: (i, 0, 0, 0)
    slab = pl.BlockSpec((None, None, HR, dk, dv), lambda i: (layer, i, 0, 0, 0))
    n_alias = 0 if s_prev is None else 1
    in_specs = [pl.BlockSpec((None, HR, dk, 1), per_b),
                pl.BlockSpec((None, HR, dk, 1), per_b),
                pl.BlockSpec((None, HR, 1, dv), per_b),
                pl.BlockSpec((None, HR, 1, dv), per_b),
                slab,
                pl.BlockSpec((HR, 1, dv), lambda i: (0, 0, 0)),
                pl.BlockSpec((HR, 1, dv), lambda i: (0, 0, 0))]
    args = [q_col, k_col, v_row, gr_row, state_all, gam, bet]
    aliases = {}
    if s_prev is not None:
        in_specs.append(pl.BlockSpec(memory_space=pl.ANY))
        args.append(s_prev)
        aliases = {7: 1}
    return pl.pallas_call(
        functools.partial(_retention_step_kernel, n_alias=n_alias),
        grid=(b,),
        in_specs=in_specs,
        out_specs=[pl.BlockSpec((None, HR, 1, dv), per_b), slab],
        out_shape=[jax.ShapeDtypeStruct((b, HR, 1, dv), BF16),
                   jax.ShapeDtypeStruct((depth, b, HR, dk, dv), F32)],
        input_output_aliases=aliases,
        compiler_params=_cparams(("arbitrary",)),
        name="retention_step",
    )(*args)


def _decode_kernel(pt_ref, lam_ref, qcol_ref, knew_ref, vnew_ref, g_ref, *rest,
                   npp, n_steps, past, out_scale):
    k_refs = rest[:npp]
    v_refs = rest[npp:2 * npp]
    o_ref = rest[2 * npp]
    wq_ref, e_ref, sc_ref, b0_ref, acc_ref, st_ref = rest[2 * npp + 1:]
    s = pl.program_id(1)
    d = HA * DV_A
    blk = npp * PAGE_SIZE
    lane = lax.broadcasted_iota(jnp.int32, (1, LANES), 1)
    slope = jnp.exp2(((lane // 2) + 1).astype(F32) * (-8.0 / HA))

    @pl.when(s == 0)
    def _():
        r = lax.broadcasted_iota(jnp.int32, (d, LANES), 0)
        n = lax.broadcasted_iota(jnp.int32, (d, LANES), 1)
        wq = jnp.where(n == r // DH_A, qcol_ref[...], 0.0).astype(BF16)
        wq_ref[...] = wq
        n2 = lax.broadcasted_iota(jnp.int32, (LANES, d), 0)
        c2 = lax.broadcasted_iota(jnp.int32, (LANES, d), 1)
        e_ref[...] = jnp.where(n2 == 2 * (c2 // DV_A), 1.0, 0.0).astype(BF16)
        kn = jnp.broadcast_to(knew_ref[...], (8, d)).astype(BF16)
        s_self = _dot(kn, wq)[0:1, :]
        st_ref[0:1, :] = s_self
        st_ref[1:2, :] = s_self
        t = lax.broadcasted_iota(jnp.int32, (PAGE_SIZE, LANES), 0)
        b0_ref[...] = (t - past).astype(F32) * slope

    @pl.when(s < n_steps)
    def _():
        m = st_ref[1:2, :]
        for r in range(npp):
            sc = None
            for h in range(HA):
                kh = k_refs[r][pl.ds(h, PAGE_SIZE, stride=HA), :].astype(BF16)
                part = _dot(kh, wq_ref[h * DV_A:(h + 1) * DV_A, :])
                sc = part if sc is None else sc + part
            row0 = pl.multiple_of(s * blk + r * PAGE_SIZE, PAGE_SIZE)
            sc = sc + b0_ref[...] + row0.astype(F32) * slope
            sc_ref[pl.ds(row0, PAGE_SIZE), :] = sc
            m = jnp.maximum(m, jnp.max(sc, axis=0, keepdims=True))
        st_ref[1:2, :] = m

    @pl.when(s == n_steps)
    def _():
        m = st_ref[1:2, :]
        ch = min(512, past)

        def sum_body(i, l):
            r0 = pl.multiple_of(i * ch, ch)
            p = jnp.exp(sc_ref[pl.ds(r0, ch), :] - m)
            sc_ref[pl.ds(r0, ch), :] = p
            return l + jnp.sum(p, axis=0, keepdims=True)

        p_self = jnp.exp(st_ref[0:1, :] - m)
        l = lax.fori_loop(0, past // ch, sum_body, p_self)
        inv = 1.0 / l
        st_ref[2:3, :] = inv
        pn_self = p_self * inv
        a_self = pn_self - lam_ref[0] * pltpu.roll(pn_self, LANES - 1, 1)
        a_self_x = _dot(jnp.broadcast_to(a_self, (8, LANES)).astype(BF16), e_ref[...])
        acc_ref[...] = jnp.zeros_like(acc_ref)
        acc_ref[0:1, :] = a_self_x[0:1, :] * vnew_ref[...]

    @pl.when(s >= n_steps)
    def _():
        inv = st_ref[2:3, :]
        lam = lam_ref[0]
        acc = [acc_ref[:, h * DV_A:(h + 1) * DV_A] for h in range(HA)]
        for r in range(npp):
            row0 = pl.multiple_of((s - n_steps) * blk + r * PAGE_SIZE, PAGE_SIZE)
            pn = sc_ref[pl.ds(row0, PAGE_SIZE), :] * inv
            a = (pn - lam * pltpu.roll(pn, LANES - 1, 1)).astype(BF16)
            for h in range(HA):
                a_x = _dot(a, e_ref[:, h * DV_A:(h + 1) * DV_A])
                prod = a_x * v_refs[r][pl.ds(h, PAGE_SIZE, stride=HA), :]
                acc[h] = acc[h] + jnp.sum(prod.reshape(PAGE_SIZE // 8, 8, DV_A), axis=0)
        for h in range(HA):
            acc_ref[:, h * DV_A:(h + 1) * DV_A] = acc[h]

    @pl.when(s == 2 * n_steps - 1)
    def _():
        o = jnp.sum(acc_ref[...], axis=0, keepdims=True)
        g = g_ref[...] * out_scale
        for h in range(HA):
            oh = o[:, h * DV_A:(h + 1) * DV_A]
            ms = jnp.mean(oh * oh, axis=-1, keepdims=True)
            o_ref[:, h * DV_A:(h + 1) * DV_A] = (oh * lax.rsqrt(ms + EPS) * g).astype(o_ref.dtype)


def _decode_attention(layer, cache_k, cache_v, page_table, lam, q_col, k_new, v_new, g_row, out_scale, npp):
    b, n_pages = page_table.shape
    d = HA * DV_A
    assert n_pages % npp == 0
    n_steps = n_pages // npp
    past = n_pages * PAGE_SIZE
    kern = functools.partial(_decode_kernel, npp=npp, n_steps=n_steps, past=past, out_scale=out_scale)

    def k_map(r):
        return lambda bi, s, pt: (layer, pt[bi, jnp.minimum(s, n_steps - 1) * npp + r], 0, 0)

    def v_map(r):
        return lambda bi, s, pt: (layer, pt[bi, jnp.maximum(s - n_steps, 0) * npp + r], 0, 0)

    page_block = (None, None, PAGE_SIZE * HA, DV_A)
    per_b = lambda bi, s, pt: (bi, 0, 0)
    grid_spec = pltpu.PrefetchScalarGridSpec(
        num_scalar_prefetch=1,
        grid=(b, 2 * n_steps),
        in_specs=[pl.BlockSpec(memory_space=pltpu.SMEM),
                  pl.BlockSpec((None, d, 1), per_b),
                  pl.BlockSpec((None, 1, d), per_b),
                  pl.BlockSpec((None, 1, d), per_b),
                  pl.BlockSpec((1, DV_A), lambda bi, s, pt: (0, 0))]
                 + [pl.BlockSpec(page_block, k_map(r)) for r in range(npp)]
                 + [pl.BlockSpec(page_block, v_map(r)) for r in range(npp)],
        out_specs=pl.BlockSpec((None, 1, d), per_b),
        scratch_shapes=[pltpu.VMEM((d, LANES), BF16), pltpu.VMEM((LANES, d), BF16),
                        pltpu.VMEM((past, LANES), F32), pltpu.VMEM((PAGE_SIZE, LANES), F32),
                        pltpu.VMEM((8, d), F32), pltpu.VMEM((8, LANES), F32)],
    )
    return pl.pallas_call(
        kern,
        grid_spec=grid_spec,
        out_shape=jax.ShapeDtypeStruct((b, 1, d), BF16),
        compiler_params=_cparams(("arbitrary", "arbitrary")),
        name="paged_decode_attn",
    )(page_table, lam, q_col, k_new, v_new, g_row, *([cache_k] * npp), *([cache_v] * npp))


def _merge_kernel(ya_ref, yr_ref, ga_ref, gb_ref, h_ref, wpa_ref, wpr_ref, wo_ref, g1_ref, b1_ref, o_ref, *, alpha):
    ya = _dot(ya_ref[...], wpa_ref[...])
    yr = _dot(yr_ref[...], wpr_ref[...])
    mix = (_sigmoid(ga_ref[...]) * ya + _sigmoid(gb_ref[...]) * yr).astype(BF16)
    x = alpha * h_ref[...] + _dot(mix, wo_ref[...])
    o_ref[...] = _layer_norm(x, g1_ref[...], b1_ref[...])


def _merge(y_a, y_r, rg, h, wpa, wpr, wo, g1, b1, alpha, tm):
    m, d = h.shape
    tm = _row_tile(m, tm)
    row = lambda i: (i, 0)
    const = lambda i: (0, 0)
    kern = functools.partial(_merge_kernel, alpha=alpha)
    return pl.pallas_call(
        kern,
        grid=(m // tm,),
        in_specs=[pl.BlockSpec((tm, d), row),
                  pl.BlockSpec((tm, 2 * d), row),
                  pl.BlockSpec((tm, d), lambda i: (i, 6)),
                  pl.BlockSpec((tm, d), lambda i: (i, 7)),
                  pl.BlockSpec((tm, d), row),
                  pl.BlockSpec((d, d), const),
                  pl.BlockSpec((2 * d, d), const),
                  pl.BlockSpec((d, d), const),
                  pl.BlockSpec((1, d), const),
                  pl.BlockSpec((1, d), const)],
        out_specs=pl.BlockSpec((tm, d), row),
        out_shape=jax.ShapeDtypeStruct((m, d), F32),
        compiler_params=_cparams(("parallel",)),
        name="branch_merge_ln1",
    )(y_a, y_r, rg, rg, h, wpa, wpr, wo, g1, b1)


def _ffn_kernel(x_ref, p_ref, win_ref, wout_ref, wg_ref, wp_ref, g2_ref, b2_ref, o_ref, *, alpha, dff, chunk):
    x = x_ref[...]
    xb = x.astype(BF16)
    acc = jnp.zeros(x.shape, F32)
    for c0 in range(0, dff, chunk):
        u = _dot(xb, win_ref[:, c0:c0 + chunk])
        g = _dot(xb, win_ref[:, dff + c0:dff + c0 + chunk])
        act = (u * _sigmoid(u) * g).astype(BF16)
        acc = acc + _dot(act, wout_ref[c0:c0 + chunk, :])
    x2 = _layer_norm(alpha * x + acc, g2_ref[...], b2_ref[...])
    gate = _sigmoid(_dot(x2.astype(BF16), wg_ref[...]))
    o_ref[...] = x2 + gate * _dot(p_ref[...].astype(BF16), wp_ref[...])


def _ffn(x, p_all, layer, win, wout, wg, wp, g2, b2, alpha, tm):
    m, d = x.shape
    dff = wout.shape[0]
    dp = p_all.shape[-1]
    tm = _row_tile(m, tm)
    chunk = dff // 2
    assert chunk % LANES == 0
    row = lambda i: (i, 0)
    const = lambda i: (0, 0)
    once = pl.Buffered(1)
    kern = functools.partial(_ffn_kernel, alpha=alpha, dff=dff, chunk=chunk)
    return pl.pallas_call(
        kern,
        grid=(m // tm,),
        in_specs=[pl.BlockSpec((tm, d), row),
                  pl.BlockSpec((None, tm, dp), lambda i: (layer, i, 0)),
                  pl.BlockSpec((d, 2 * dff), const, pipeline_mode=once),
                  pl.BlockSpec((dff, d), const, pipeline_mode=once),
                  pl.BlockSpec((d, d), const, pipeline_mode=once),
                  pl.BlockSpec((dp, d), const, pipeline_mode=once),
                  pl.BlockSpec((1, d), const),
                  pl.BlockSpec((1, d), const)],
        out_specs=pl.BlockSpec((tm, d), row),
        out_shape=jax.ShapeDtypeStruct((m, d), F32),
        compiler_params=_cparams(("parallel",)),
        name="ffn_ln2_ple",
    )(x, p_all, win, wout, wg, wp, g2, b2)


def kernel(x_prompt, x_sample, cache_k, cache_v, state_ret, page_table, p_prompt, p_sample, w_in, lambda_q1, lambda_k1, lambda_q2, lambda_k2, attn_norm_g, ret_norm_g, ret_norm_b, w_pa, w_pr, w_o, ln1_g, ln1_b, ln2_g, ln2_b, w_ffn_in, w_ffn_out, w_ple_gate, w_ple_proj):
    depth = w_in.shape[0]
    bp, seq, d = x_prompt.shape
    db, dseq, _ = x_sample.shape
    assert bp == 1 and dseq == 1 and d == HA * DV_A
    dk = d // HR
    dv = 2 * d // HR
    alpha = (2 * depth) ** 0.25
    n_phys = cache_k.shape[1]

    slopes = jnp.asarray((2.0 ** (-8.0 * (np.arange(HA) + 1.0) / HA)).astype(np.float32))
    flash_t = min(seq, 1024)
    s2 = (2.0 ** (-8.0 * (np.arange(HA) + 1.0) / HA)).astype(np.float32) * np.float32(LOG2E)
    top8 = lambda x: (x.view(np.uint32) & np.uint32(0xFFFF0000)).view(np.float32)
    a1 = top8(s2)
    a2 = top8(s2 - a1)
    a3 = top8(s2 - a1 - a2)
    qfeat = np.zeros((HA, LANES), np.float32)
    qfeat[:, DH_A:DH_A + 6] = np.stack([16.0 * a1, 16.0 * a2, 16.0 * a3, a1, a2, a3], axis=1)
    qfeat = jnp.asarray(qfeat)
    log_g = jnp.log(1.0 - 2.0 ** (-5.0 - jnp.arange(HR, dtype=F32)))
    ret_chunk = min(seq, 256)
    ret_consts = jnp.stack([log_g, jnp.exp(log_g * ret_chunk)])
    rg_scale = jnp.concatenate([jnp.ones((d,), F32), jnp.full((d,), dk ** -0.5, F32),
                                jnp.ones((6 * d,), F32)]).reshape(1, 8 * d)
    cache_k2 = cache_k.reshape(depth, n_phys, PAGE_SIZE * HA, DV_A)
    cache_v2 = cache_v.reshape(depth, n_phys, PAGE_SIZE * HA, DV_A)
    page_table = page_table.astype(jnp.int32)

    hp = x_prompt.reshape(seq, d)
    hs = x_sample.reshape(db, d)
    rp_l = []
    kvp = kvs = rs_all = None
    pp_all = p_prompt.reshape(depth, seq, -1)
    ps_all = p_sample.reshape(depth, db, -1)
    for i in range(depth):
        lam_init = 0.8 - 0.6 * math.exp(-0.3 * i)
        lam = (jnp.exp(jnp.sum(lambda_q1[i].astype(F32) * lambda_k1[i].astype(F32)))
               - jnp.exp(jnp.sum(lambda_q2[i].astype(F32) * lambda_k2[i].astype(F32)))
               + lam_init).reshape(1)
        out_scale = 1.0 - lam_init
        w_in_bf = w_in[i].astype(BF16)
        wpa, wpr, wo = w_pa[i].astype(BF16), w_pr[i].astype(BF16), w_o[i].astype(BF16)
        wfi, wfo = w_ffn_in[i].astype(BF16), w_ffn_out[i].astype(BF16)
        wpg, wpp = w_ple_gate[i].astype(BF16), w_ple_proj[i].astype(BF16)
        g1, b1 = ln1_g[i].reshape(1, d), ln1_b[i].reshape(1, d)
        g2, b2 = ln2_g[i].reshape(1, d), ln2_b[i].reshape(1, d)
        gam, bet = ret_norm_g[i].reshape(1, 2 * d), ret_norm_b[i].reshape(1, 2 * d)
        attn_g = attn_norm_g[i]

        q_bf, kp_all, vp_all, k_bf, vt_bf = _qkv_proj(hp, w_in_bf, 512, i, depth, kvp, qfeat, flash_t)
        kvp = (kp_all, vp_all)
        rg = _proj(hp, w_in_bf, rg_scale, 3 * d, 8 * d, tm=1024, tn=1024)
        y_a = _flash_attention(q_bf, k_bf, vt_bf, slopes, lam, attn_g.reshape(DV_A, 1), out_scale, t=flash_t)
        y_r, ret_p = _retention_prompt(rg, ret_consts, gam, bet, L=ret_chunk)
        x1 = _merge(y_a, y_r, rg, hp, wpa, wpr, wo, g1, b1, alpha, tm=512)
        hp = _ffn(x1, pp_all, i, wfi, wfo, wpg, wpp, g2, b2, alpha, tm=512)
        rp_l.append(ret_p.reshape(1, HR, dk, dv))

        q_s, ks_all, vs_all = _qkv_proj(hs, w_in_bf, db, i, depth, kvs)
        kvs = (ks_all, vs_all)
        rg_s = _proj(hs, w_in_bf, rg_scale, 3 * d, 8 * d, tm=db, tn=1024)
        ya_s = _decode_attention(i, cache_k2, cache_v2, page_table, lam, q_s.reshape(db, d, 1),
                                 ks_all[i].reshape(db, 1, d), vs_all[i].reshape(db, 1, d),
                                 attn_g.reshape(1, DV_A), out_scale, npp=min(16, page_table.shape[1]))
        yr_s, rs_all = _retention_step(rg_s[:, 0:d].reshape(db, HR, dk, 1),
                                       rg_s[:, d:2 * d].reshape(db, HR, dk, 1),
                                       rg_s[:, 2 * d:4 * d].reshape(db, HR, 1, dv),
                                       rg_s[:, 4 * d:6 * d].reshape(db, HR, 1, dv),
                                       state_ret, gam.reshape(HR, 1, dv), bet.reshape(HR, 1, dv), i, rs_all)
        x1_s = _merge(ya_s.reshape(db, d), yr_s.reshape(db, 2 * d), rg_s, hs, wpa, wpr, wo, g1, b1, alpha, tm=db)
        hs = _ffn(x1_s, ps_all, i, wfi, wfo, wpg, wpp, g2, b2, alpha, tm=db)

    return (hp.reshape(1, seq, d), hs.reshape(db, 1, d),
            kvp[0].reshape(depth, 1, seq, HA, DV_A), kvp[1].reshape(depth, 1, seq, HA, DV_A), jnp.stack(rp_l),
            kvs[0].reshape(depth, db, 1, HA, DV_A), kvs[1].reshape(depth, db, 1, HA, DV_A), rs_all)
```
